```python
import jax, jax.numpy as jnp
from jax import lax
import numpy as np

D_MODEL = 2048
BATCH = 4
SEQ = 8192
DEPTH = 2

CHUNK = 64
Q_BLOCK = 2 * CHUNK
N_MIXERS = 2
N_POOL_LAYERS = (DEPTH + N_MIXERS - 1) // N_MIXERS
N_FOX_LAYERS = DEPTH // N_MIXERS
POOL_WINDOWS = (2, 4, 8, 16)
N_POOL_GROUPS = 4
POOL_GROUP_DIM = D_MODEL // N_POOL_GROUPS
FOX_HEAD_DIM = 128
FOX_HEADS = D_MODEL // FOX_HEAD_DIM
FORGET_BIAS_INIT = 3.0
N_GROUPS = 8
EXPERTS_PER_GROUP = 8
N_EXPERTS = N_GROUPS * EXPERTS_PER_GROUP
D_EXPERT = D_MODEL // 4
TOP_K = 2
ALPHA = (2 * DEPTH) ** 0.25
BETA = (8 * DEPTH) ** -0.25
LN_EPS = 1e-5

kernel_name = "hybrid_pool_fox_hmoe_deepnorm_adaln"


def layer_norm(x, g, b):
    xf = x.astype(jnp.float32)
    mu = jnp.mean(xf, axis=-1, keepdims=True)
    var = jnp.mean(jnp.square(xf - mu), axis=-1, keepdims=True)
    y = (xf - mu) * lax.rsqrt(var + LN_EPS) * g.astype(jnp.float32) + b.astype(jnp.float32)
    return y.astype(x.dtype)


def pool_mixer(u, w_pool, scale):
    B, S, D = u.shape
    f32 = jnp.float32
    cs = jnp.cumsum(u.astype(f32), axis=1)
    cs = jnp.concatenate([jnp.zeros((B, 1, D), f32), cs], axis=1)
    t = jnp.arange(S)
    groups = []
    for g, w in enumerate(POOL_WINDOWS):
        sl = slice(g * POOL_GROUP_DIM, (g + 1) * POOL_GROUP_DIM)
        lo = jnp.maximum(t + 1 - w, 0)
        win_sum = cs[:, 1:, sl] - jnp.take(cs[:, :, sl], lo, axis=1)
        count = jnp.minimum(t + 1, w).astype(f32)[None, :, None]
        groups.append(win_sum / count - u[..., sl].astype(f32))
    pooled = jnp.stack(groups, axis=2).astype(u.dtype)
    mixed = jnp.einsum('bsgc,gcd->bsgd', pooled, w_pool).reshape(B, S, D)
    return mixed * scale


def forgetting_attention(u, w_in, b_f, w_o):
    B, S, D = u.shape
    H, Dh = FOX_HEADS, FOX_HEAD_DIM
    f32 = jnp.float32
    proj = u @ w_in
    q, k, v, f_logit = jnp.split(proj, [D, 2 * D, 3 * D], axis=-1)
    q = q.reshape(B, S, H, Dh).transpose(0, 2, 1, 3)
    k = k.reshape(B, S, H, Dh).transpose(0, 2, 1, 3)
    v = v.reshape(B, S, H, Dh).transpose(0, 2, 1, 3)
    log_f = jax.nn.log_sigmoid((f_logit + b_f).astype(f32))
    cum = jnp.cumsum(log_f, axis=1).transpose(0, 2, 1)
    nq = S // Q_BLOCK
    q_blocks = q.reshape(B, H, nq, Q_BLOCK, Dh).transpose(2, 0, 1, 3, 4)
    c_blocks = cum.reshape(B, H, nq, Q_BLOCK).transpose(2, 0, 1, 3)
    pos_blocks = jnp.arange(S).reshape(nq, Q_BLOCK)
    k_pos = jnp.arange(S)
    sm_scale = Dh ** -0.5

    def attend_block(args):
        qb, cq, qpos = args
        logits = jnp.einsum('bhqd,bhkd->bhqk', qb, k, preferred_element_type=f32) * sm_scale
        logits = logits + cq[..., :, None] - cum[:, :, None, :]
        mask = k_pos[None, :] <= qpos[:, None]
        logits = jnp.where(mask, logits, -jnp.inf)
        p = jax.nn.softmax(logits, axis=-1)
        return jnp.einsum('bhqk,bhkd->bhqd', p.astype(v.dtype), v)

    out = lax.map(attend_block, (q_blocks, c_blocks, pos_blocks))
    out = out.transpose(1, 0, 3, 2, 4).reshape(B, S, D)
    return out @ w_o


def hierarchical_moe(u, w_rg, b_rg, w_re, b_re, w_gu, w_down):
    B, S, D = u.shape
    T = B * S
    f32 = jnp.float32
    h = u.reshape(T, D)
    g_logits = (h @ w_rg).astype(f32) + b_rg.astype(f32)
    g_probs = jax.nn.softmax(g_logits, axis=-1)
    g_idx = jnp.argmax(g_logits, axis=-1)
    p_group = jnp.take_along_axis(g_probs, g_idx[:, None], axis=-1)
    e_logits = ((h @ w_re).astype(f32) + b_re.astype(f32)).reshape(T, N_GROUPS, EXPERTS_PER_GROUP)
    e_logits = jnp.take_along_axis(e_logits, g_idx[:, None, None], axis=1)[:, 0]
    top_vals, top_idx = lax.top_k(e_logits, TOP_K)
    weights = jax.nn.softmax(top_vals, axis=-1) * p_group
    expert_ids = g_idx[:, None] * EXPERTS_PER_GROUP + top_idx
    flat_e = expert_ids.reshape(-1)
    flat_tok = jnp.repeat(jnp.arange(T), TOP_K)
    flat_w = weights.reshape(-1)
    order = jnp.argsort(flat_e)
    tok_sorted = flat_tok[order]
    group_sizes = jnp.bincount(flat_e, length=N_EXPERTS).astype(jnp.int32)
    xs = h[tok_sorted]
    gu = lax.ragged_dot(xs, w_gu, group_sizes)
    gate, up = jnp.split(gu, 2, axis=-1)
    act = jax.nn.silu(gate) * up
    out = lax.ragged_dot(act, w_down, group_sizes)
    out = out * flat_w[order][:, None].astype(out.dtype)
    y = jax.ops.segment_sum(out, tok_sorted, num_segments=T)
    return y.reshape(B, S, D)


def setup_inputs(seed: int = 0) -> dict:
    key = jax.random.key(seed)
    ks = jax.random.split(key, 17)
    f32 = jnp.float32
    D, H, C = D_MODEL, FOX_HEADS, POOL_GROUP_DIM

    def nrm(k, shape, s):
        return jax.random.normal(k, shape, f32) * s

    x = nrm(ks[0], (BATCH, SEQ, D), 1.0)
    c = nrm(ks[1], (BATCH, D), 1.0)
    ada_w = nrm(ks[2], (DEPTH, D, 6 * D), 0.1 * D ** -0.5)
    ada_b = nrm(ks[3], (DEPTH, 6 * D), 0.01)
    ln_g = 1.0 + nrm(ks[4], (DEPTH, 2, D), 0.02)
    ln_b = nrm(ks[5], (DEPTH, 2, D), 0.02)
    pool_w = nrm(ks[6], (N_POOL_LAYERS, N_POOL_GROUPS, C, C), BETA * C ** -0.5)
    pool_scale = 1.0 + nrm(ks[7], (N_POOL_LAYERS, D), 0.02)
    col_scale = jnp.concatenate([jnp.ones((2 * D,), f32), jnp.full((D,), BETA, f32), jnp.ones((H,), f32)])
    fox_w_in = nrm(ks[8], (N_FOX_LAYERS, D, 3 * D + H), D ** -0.5) * col_scale
    fox_b_f = FORGET_BIAS_INIT + nrm(ks[9], (N_FOX_LAYERS, H), 0.5)
    fox_w_o = nrm(ks[10], (N_FOX_LAYERS, D, D), BETA * D ** -0.5)
    router_w_group = nrm(ks[11], (DEPTH, D, N_GROUPS), D ** -0.5)
    router_b_group = nrm(ks[12], (DEPTH, N_GROUPS), 0.01)
    router_w_expert = nrm(ks[13], (DEPTH, D, N_EXPERTS), D ** -0.5)
    router_b_expert = nrm(ks[14], (DEPTH, N_EXPERTS), 0.01)
    moe_w_gate_up = nrm(ks[15], (DEPTH, N_EXPERTS, D, 2 * D_EXPERT), D ** -0.5)
    moe_w_down = nrm(ks[16], (DEPTH, N_EXPERTS, D_EXPERT, D), BETA * D_EXPERT ** -0.5)
    return {"x": x, "c": c, "ada_w": ada_w, "ada_b": ada_b, "ln_g": ln_g, "ln_b": ln_b,
            "pool_w": pool_w, "pool_scale": pool_scale, "fox_w_in": fox_w_in, "fox_b_f": fox_b_f,
            "fox_w_o": fox_w_o, "router_w_group": router_w_group, "router_b_group": router_b_group,
            "router_w_expert": router_w_expert, "router_b_expert": router_b_expert,
            "moe_w_gate_up": moe_w_gate_up, "moe_w_down": moe_w_down}


def reference(x, c, ada_w, ada_b, ln_g, ln_b, pool_w, pool_scale, fox_w_in, fox_b_f, fox_w_o,
              router_w_group, router_b_group, router_w_expert, router_b_expert,
              moe_w_gate_up, moe_w_down):
    c_act = jax.nn.silu(c)
    for l in range(DEPTH):
        mod = c_act @ ada_w[l] + ada_b[l]
        shift_t, scale_t, gate_t, shift_c, scale_c, gate_c = [m[:, None, :] for m in jnp.split(mod, 6, axis=-1)]
        u = x * (1 + scale_t) + shift_t
        j = l // N_MIXERS
        if l % N_MIXERS == 0:
            y = pool_mixer(u, pool_w[j], pool_scale[j])
        else:
            y = forgetting_attention(u, fox_w_in[j], fox_b_f[j], fox_w_o[j])
        x = layer_norm(ALPHA * x + (1 + gate_t) * y, ln_g[l, 0], ln_b[l, 0])
        u = x * (1 + scale_c) + shift_c
        y = hierarchical_moe(u, router_w_group[l], router_b_group[l], router_w_expert[l],
                             router_b_expert[l], moe_w_gate_up[l], moe_w_down[l])
        x = layer_norm(ALPHA * x + (1 + gate_c) * y, ln_g[l, 1], ln_b[l, 1])
    return x
```

```python
import functools

import jax
import jax.numpy as jnp
from jax import lax
from jax.experimental import pallas as pl
from jax.experimental.pallas import tpu as pltpu

F32 = jnp.float32
BF16 = jnp.bfloat16
I32 = jnp.int32
HIGHEST = lax.Precision.HIGHEST

DEPTH = 2
POOL_WINDOWS = (2, 4, 8, 16)
POOL_HALO = 16
HEAD_DIM = 128
N_GROUPS = 8
EXPERTS_PER_GROUP = 8
N_EXPERTS = N_GROUPS * EXPERTS_PER_GROUP
TOP_K = 2
ALPHA = (2 * DEPTH) ** 0.25
LN_EPS = 1e-5

LANES = 128
V7X_VMEM_BYTES = 64 * 1024 * 1024
VMEM_LIMIT_BYTES = V7X_VMEM_BYTES - 8 * 1024 * 1024

META_E1, META_E2, META_W1, META_W2, META_R1, META_R2 = range(6)
ROUTER_LANE0 = N_GROUPS


def _params(*semantics):
    return pltpu.CompilerParams(dimension_semantics=semantics, vmem_limit_bytes=VMEM_LIMIT_BYTES)


def _tile(n, pref):
    t = min(n, pref)
    while n % t:
        t //= 2
    return t


def _adaln_kernel(c_ref, w_ref, b_ref, o_ref):
    c = c_ref[...]
    c_act = c * (1.0 / (1.0 + jnp.exp(-c)))
    o_ref[0] = jnp.dot(c_act, w_ref[0], precision=HIGHEST, preferred_element_type=F32) + b_ref[0]


def _adaln(c, ada_w, ada_b):
    n_layers, d, n = ada_w.shape
    b = c.shape[0]
    rows = -(-b // 8) * 8
    c_pad = jnp.pad(c, ((0, rows - b), (0, 0)))
    tn = _tile(n, 1024)
    out = pl.pallas_call(
        _adaln_kernel,
        grid=(n_layers, n // tn),
        in_specs=[
            pl.BlockSpec((rows, d), lambda l, j: (0, 0)),
            pl.BlockSpec((1, d, tn), lambda l, j: (l, 0, j)),
            pl.BlockSpec((1, 1, tn), lambda l, j: (l, 0, j)),
        ],
        out_specs=pl.BlockSpec((1, rows, tn), lambda l, j: (l, 0, j)),
        out_shape=jax.ShapeDtypeStruct((n_layers, rows, n), F32),
        compiler_params=_params("arbitrary", "arbitrary"),
        name="adaln_mod",
    )(c_pad, ada_w, ada_b.reshape(n_layers, 1, n))
    return out[:, :b].reshape(n_layers, b, 6, d)


def _layer_norm(h, g, b):
    mu = jnp.mean(h, axis=-1, keepdims=True)
    d = h - mu
    var = jnp.mean(d * d, axis=-1, keepdims=True)
    return d * lax.rsqrt(var + LN_EPS) * g + b


def _pack_bf16_pairs(u_hi):
    half = u_hi.shape[1] // 2
    bits = lax.bitcast_convert_type(u_hi.astype(F32), I32)
    return bits[:, :half] | lax.shift_right_logical(bits[:, half:], 16)


def _unpack_bf16_pairs(words):
    hi = lax.bitcast_convert_type(words & jnp.int32(-65536), F32).astype(BF16)
    lo = lax.bitcast_convert_type(lax.shift_left(words, 16), F32).astype(BF16)
    return hi, lo


def _route(logits, cnt_ref):
    rows = logits.shape[0]
    lane = lax.broadcasted_iota(I32, (rows, LANES), 1).astype(F32)
    neg_inf = jnp.float32(-jnp.inf)
    big = jnp.float32(2 * LANES)

    g_mask = lane < N_GROUPS
    gl = jnp.where(g_mask, logits, neg_inf)
    g_max = jnp.max(gl, axis=-1, keepdims=True)
    g_idx = jnp.min(jnp.where(gl == g_max, lane, big), axis=-1, keepdims=True)
    p_group = 1.0 / jnp.sum(jnp.where(g_mask, jnp.exp(gl - g_max), 0.0), axis=-1, keepdims=True)

    lo = ROUTER_LANE0 + EXPERTS_PER_GROUP * g_idx
    e_mask = (lane >= lo) & (lane < lo + EXPERTS_PER_GROUP)
    el = jnp.where(e_mask, logits, neg_inf)
    v1 = jnp.max(el, axis=-1, keepdims=True)
    i1 = jnp.min(jnp.where(el == v1, lane, big), axis=-1, keepdims=True)
    el2 = jnp.where(lane == i1, neg_inf, el)
    v2 = jnp.max(el2, axis=-1, keepdims=True)
    i2 = jnp.min(jnp.where(el2 == v2, lane, big), axis=-1, keepdims=True)
    t = jnp.exp(v2 - v1)
    w1 = p_group / (1.0 + t)
    w2 = p_group * t / (1.0 + t)

    sel1 = lane == i1
    sel2 = lane == i2
    onehot = jnp.where(sel1 | sel2, 1.0, 0.0)
    r_idx = lax.broadcasted_iota(I32, (rows, rows), 0)
    c_idx = lax.broadcasted_iota(I32, (rows, rows), 1)
    strict_lower = jnp.where(c_idx < r_idx, 1.0, 0.0).astype(BF16)
    before = jnp.dot(strict_lower, onehot.astype(BF16), preferred_element_type=F32) + cnt_ref[...]
    r1 = jnp.sum(jnp.where(sel1, before, 0.0), axis=-1, keepdims=True)
    r2 = jnp.sum(jnp.where(sel2, before, 0.0), axis=-1, keepdims=True)
    cnt_ref[...] += jnp.sum(onehot, axis=0, keepdims=True)

    cols = (i1 - ROUTER_LANE0, i2 - ROUTER_LANE0, w1, w2, r1, r2)
    meta = jnp.zeros((rows, LANES), F32)
    for k, col in enumerate(cols):
        meta = jnp.where(lane == k, col, meta)
    return meta


def _post_mixer(x, y, mod, lng, lnb, wr_hi, wr_lo, br, x1_ref, up_ref, meta_ref, cnt_ref):
    gate_t, shift_c, scale_c = mod[2:3], mod[3:4], mod[4:5]
    x1 = _layer_norm(ALPHA * x + (1.0 + gate_t) * y, lng, lnb)
    x1_ref[...] = x1
    u = x1 * (1.0 + scale_c) + shift_c
    u_hi = u.astype(BF16)
    u_lo = (u - u_hi.astype(F32)).astype(BF16)
    up_ref[...] = _pack_bf16_pairs(u_hi)
    logits = (jnp.dot(u_hi, wr_hi, preferred_element_type=F32)
              + jnp.dot(u_lo, wr_hi, preferred_element_type=F32)
              + jnp.dot(u_hi, wr_lo, preferred_element_type=F32)) + br
    meta_ref[...] = _route(logits, cnt_ref)


def _router_weights(w_rg, b_rg, w_re, b_re):
    d = w_rg.shape[0]
    pad = LANES - N_GROUPS - N_EXPERTS
    w = jnp.concatenate([w_rg, w_re, jnp.zeros((d, pad), F32)], axis=1)
    b = jnp.concatenate([b_rg, b_re, jnp.zeros((pad,), F32)]).reshape(1, LANES)
    w_hi = w.astype(BF16)
    w_lo = (w - w_hi.astype(F32)).astype(BF16)
    return w_hi, w_lo, b


def _pool_kernel(x_ref, halo_ref, mod_ref, wp_ref, ps_ref, lng_ref, lnb_ref, wrh_ref, wrl_ref, br_ref,
                 x1_ref, up_ref, meta_ref, cnt_ref):
    b = pl.program_id(0)
    i = pl.program_id(1)

    @pl.when((b == 0) & (i == 0))
    def _():
        cnt_ref[...] = jnp.zeros_like(cnt_ref)

    mod = mod_ref[0]
    shift_t, scale_t = mod[0:1], mod[1:2]
    x = x_ref[0]
    rows, d = x.shape
    group = d // len(POOL_WINDOWS)
    u = x * (1.0 + scale_t) + shift_t
    halo = halo_ref[0] * (1.0 + scale_t) + shift_t
    halo = jnp.where(i > 0, halo, 0.0)
    pos = i * rows + lax.broadcasted_iota(I32, (rows, 1), 0)

    ys = []
    for g, w in enumerate(POOL_WINDOWS):
        cols = slice(g * group, (g + 1) * group)
        s = jnp.concatenate([halo[:, cols], u[:, cols]], axis=0)
        k = 1
        while k < w:
            s = s + pltpu.roll(s, k, axis=0)
            k *= 2
        inv_count = 1.0 / jnp.minimum(pos + 1, w).astype(F32)
        pooled = s[POOL_HALO:, :] * inv_count - u[:, cols]
        mixed = jnp.dot(pooled.astype(BF16), wp_ref[g], preferred_element_type=F32)
        ys.append(mixed * ps_ref[:, cols])
    y = jnp.concatenate(ys, axis=-1)

    _post_mixer(x, y, mod, lng_ref[...], lnb_ref[...], wrh_ref[...], wrl_ref[...], br_ref[...],
                x1_ref.at[0], up_ref, meta_ref, cnt_ref)


def _pool_layer(x, mod, pool_w, pool_scale, lng, lnb, router):
    bsz, s, d = x.shape
    ts = _tile(s, 256)
    nsb = s // ts
    hb = ts // POOL_HALO
    wr_hi, wr_lo, br = router
    vec = lambda b, i: (0, 0)
    return pl.pallas_call(
        _pool_kernel,
        grid=(bsz, nsb),
        in_specs=[
            pl.BlockSpec((1, ts, d), lambda b, i: (b, i, 0)),
            pl.BlockSpec((1, POOL_HALO, d), lambda b, i: (b, jnp.maximum(i * hb - 1, 0), 0)),
            pl.BlockSpec((1, 6, d), lambda b, i: (b, 0, 0)),
            pl.BlockSpec(pool_w.shape, lambda b, i: (0, 0, 0)),
            pl.BlockSpec((1, d), vec),
            pl.BlockSpec((1, d), vec),
            pl.BlockSpec((1, d), vec),
            pl.BlockSpec((d, LANES), vec),
            pl.BlockSpec((d, LANES), vec),
            pl.BlockSpec((1, LANES), vec),
        ],
        out_specs=[
            pl.BlockSpec((1, ts, d), lambda b, i: (b, i, 0)),
            pl.BlockSpec((ts, d // 2), lambda b, i: (b * nsb + i, 0)),
            pl.BlockSpec((ts, LANES), lambda b, i: (b * nsb + i, 0)),
            pl.BlockSpec((1, LANES), vec),
        ],
        out_shape=[
            jax.ShapeDtypeStruct((bsz, s, d), F32),
            jax.ShapeDtypeStruct((bsz * s, d // 2), I32),
            jax.ShapeDtypeStruct((bsz * s, LANES), F32),
            jax.ShapeDtypeStruct((1, LANES), F32),
        ],
        compiler_params=_params("arbitrary", "arbitrary"),
        name="pool_layer",
    )(x, x, mod, pool_w.astype(BF16), pool_scale.reshape(1, d), lng.reshape(1, d), lnb.reshape(1, d),
      wr_hi, wr_lo, br)


def _fox_proj_kernel(x_ref, mod_ref, w_ref, wf_ref, bf_ref, qkv_ref, cum_ref, u_scr, carry_scr, *, tiles_per_seq):
    i = pl.program_id(0)
    j = pl.program_id(1)

    @pl.when((j == 0) & (i % tiles_per_seq == 0))
    def _():
        carry_scr[...] = jnp.zeros_like(carry_scr)

    @pl.when(j == 0)
    def _():
        mod = mod_ref[0]
        u = (x_ref[...] * (1.0 + mod[1:2]) + mod[0:1]).astype(BF16)
        u_scr[...] = u
        f_logit = jnp.dot(u, wf_ref[...], preferred_element_type=F32) + bf_ref[...]
        log_f = jnp.minimum(f_logit, 0.0) - jnp.log1p(jnp.exp(-jnp.abs(f_logit)))
        rows = log_f.shape[0]
        r_idx = lax.broadcasted_iota(I32, (rows, rows), 0)
        c_idx = lax.broadcasted_iota(I32, (rows, rows), 1)
        lower = jnp.where(c_idx <= r_idx, 1.0, 0.0)
        cum = jnp.dot(lower, log_f, precision=HIGHEST, preferred_element_type=F32) + carry_scr[...]
        cum_ref[...] = cum
        carry_scr[...] = cum[rows - 1:rows, :]

    qkv_ref[...] = jnp.dot(u_scr[...], w_ref[...], preferred_element_type=F32).astype(BF16)


def _fox_proj(x2d, mod, w_qkv, w_f, b_f, seq):
    t, d = x2d.shape
    n = w_qkv.shape[1]
    tm = _tile(seq, 512)
    tn = _tile(n, 1536)
    tiles_per_seq = seq // tm
    return pl.pallas_call(
        functools.partial(_fox_proj_kernel, tiles_per_seq=tiles_per_seq),
        grid=(t // tm, n // tn),
        in_specs=[
            pl.BlockSpec((tm, d), lambda i, j: (i, 0)),
            pl.BlockSpec((1, 6, d), lambda i, j: (i // tiles_per_seq, 0, 0)),
            pl.BlockSpec((d, tn), lambda i, j: (0, j)),
            pl.BlockSpec((d, LANES), lambda i, j: (0, 0)),
            pl.BlockSpec((1, LANES), lambda i, j: (0, 0)),
        ],
        out_specs=[
            pl.BlockSpec((tm, tn), lambda i, j: (i, j)),
            pl.BlockSpec((tm, LANES), lambda i, j: (i, 0)),
        ],
        out_shape=[
            jax.ShapeDtypeStruct((t, n), BF16),
            jax.ShapeDtypeStruct((t, LANES), F32),
        ],
        scratch_shapes=[pltpu.VMEM((tm, d), BF16), pltpu.VMEM((1, LANES), F32)],
        compiler_params=_params("arbitrary", "arbitrary"),
        name="fox_proj",
    )(x2d, mod, w_qkv, w_f, b_f)


def _fox_attn_kernel(q_ref, k_ref, v_ref, cq_ref, ck_ref, o_ref, *, tk):
    h = pl.program_id(1)
    qi = pl.program_id(2)
    q = q_ref[...]
    tq = q.shape[0]
    lane = lax.broadcasted_iota(I32, (tq, LANES), 1)
    cq = jnp.sum(jnp.where(lane == h, cq_ref[...], 0.0), axis=-1, keepdims=True)
    chunks_per_q = tq // tk

    def step(kc, carry, masked):
        m, l, acc = carry
        start = pl.multiple_of(kc * tk, tk)
        k = k_ref[pl.ds(start, tk), :]
        v = v_ref[pl.ds(start, tk), :]
        ck = ck_ref[0, :, pl.ds(start, tk)]
        s = lax.dot_general(q, k, (((1,), (1,)), ((), ())), preferred_element_type=F32)
        s = s + cq - ck
        if masked:
            q_pos = qi * tq + lax.broadcasted_iota(I32, (tq, tk), 0)
            k_pos = start + lax.broadcasted_iota(I32, (tq, tk), 1)
            s = jnp.where(k_pos <= q_pos, s, -jnp.inf)
        m_new = jnp.maximum(m, jnp.max(s, axis=-1, keepdims=True))
        alpha = jnp.exp(m - m_new)
        p = jnp.exp(s - m_new)
        l = alpha * l + jnp.sum(p, axis=-1, keepdims=True)
        acc = alpha * acc + jnp.dot(p.astype(BF16), v, preferred_element_type=F32)
        return m_new, l, acc

    init = (jnp.full((tq, 1), -jnp.inf, F32), jnp.zeros((tq, 1), F32), jnp.zeros((tq, HEAD_DIM), F32))
    n_full = qi * chunks_per_q
    carry = lax.fori_loop(0, n_full, lambda kc, c: step(kc, c, False), init)
    for d in range(chunks_per_q):
        carry = step(n_full + d, carry, True)
    _, l, acc = carry
    o_ref[...] = (acc / l).astype(BF16)


def _fox_attn(qkv, cum, cum_t, bsz, seq, d):
    heads = d // HEAD_DIM
    tq = _tile(seq, 512)
    tk = _tile(tq, 512)
    nq = seq // tq
    return pl.pallas_call(
        functools.partial(_fox_attn_kernel, tk=tk),
        grid=(bsz, heads, nq),
        in_specs=[
            pl.BlockSpec((tq, HEAD_DIM), lambda b, h, i: (b * nq + i, h)),
            pl.BlockSpec((seq, HEAD_DIM), lambda b, h, i: (b, heads + h)),
            pl.BlockSpec((seq, HEAD_DIM), lambda b, h, i: (b, 2 * heads + h)),
            pl.BlockSpec((tq, LANES), lambda b, h, i: (b * nq + i, 0)),
            pl.BlockSpec((1, 1, seq), lambda b, h, i: (b * heads + h, 0, 0)),
        ],
        out_specs=pl.BlockSpec((tq, HEAD_DIM), lambda b, h, i: (b * nq + i, h)),
        out_shape=jax.ShapeDtypeStruct((bsz * seq, d), BF16),
        compiler_params=_params("arbitrary", "arbitrary", "arbitrary"),
        name="fox_attn",
    )(qkv, qkv, qkv, cum, cum_t)


def _fox_out_kernel(a_ref, x_ref, mod_ref, wo_ref, lng_ref, lnb_ref, wrh_ref, wrl_ref, br_ref,
                    x1_ref, up_ref, meta_ref, cnt_ref):
    @pl.when(pl.program_id(0) == 0)
    def _():
        cnt_ref[...] = jnp.zeros_like(cnt_ref)

    y = jnp.dot(a_ref[...], wo_ref[...], preferred_element_type=F32)
    _post_mixer(x_ref[...], y, mod_ref[0], lng_ref[...], lnb_ref[...], wrh_ref[...], wrl_ref[...],
                br_ref[...], x1_ref, up_ref, meta_ref, cnt_ref)


def _fox_out(attn, x2d, mod, w_o, lng, lnb, router, seq):
    t, d = x2d.shape
    tm = _tile(seq, 256)
    tiles_per_seq = seq // tm
    wr_hi, wr_lo, br = router
    vec = lambda i: (0, 0)
    return pl.pallas_call(
        _fox_out_kernel,
        grid=(t // tm,),
        in_specs=[
            pl.BlockSpec((tm, d), lambda i: (i, 0)),
            pl.BlockSpec((tm, d), lambda i: (i, 0)),
            pl.BlockSpec((1, 6, d), lambda i: (i // tiles_per_seq, 0, 0)),
            pl.BlockSpec((d, d), vec),
            pl.BlockSpec((1, d), vec),
            pl.BlockSpec((1, d), vec),
            pl.BlockSpec((d, LANES), vec),
            pl.BlockSpec((d, LANES), vec),
            pl.BlockSpec((1, LANES), vec),
        ],
        out_specs=[
            pl.BlockSpec((tm, d), lambda i: (i, 0)),
            pl.BlockSpec((tm, d // 2), lambda i: (i, 0)),
            pl.BlockSpec((tm, LANES), lambda i: (i, 0)),
            pl.BlockSpec((1, LANES), vec),
        ],
        out_shape=[
            jax.ShapeDtypeStruct((t, d), F32),
            jax.ShapeDtypeStruct((t, d // 2), I32),
            jax.ShapeDtypeStruct((t, LANES), F32),
            jax.ShapeDtypeStruct((1, LANES), F32),
        ],
        compiler_params=_params("arbitrary"),
        name="fox_out",
    )(attn, x2d, mod, w_o, lng.reshape(1, d), lnb.reshape(1, d), wr_hi, wr_lo, br)


def _fox_layer(x2d, mod, w_in, b_f, w_o, lng, lnb, router, bsz, seq):
    t, d = x2d.shape
    heads = d // HEAD_DIM
    sm_scale = HEAD_DIM ** -0.5
    w_qkv = jnp.concatenate([w_in[:, :d] * sm_scale, w_in[:, d:3 * d]], axis=1).astype(BF16)
    w_f = jnp.pad(w_in[:, 3 * d:], ((0, 0), (0, LANES - heads))).astype(BF16)
    b_f = jnp.pad(b_f, (0, LANES - heads)).reshape(1, LANES)
    qkv, cum = _fox_proj(x2d, mod, w_qkv, w_f, b_f, seq)
    cum_t = cum[:, :heads].reshape(bsz, seq, heads).transpose(0, 2, 1).reshape(bsz * heads, 1, seq)
    attn = _fox_attn(qkv, cum, cum_t, bsz, seq, d)
    return _fox_out(attn, x2d, mod, w_o.astype(BF16), lng, lnb, router, seq)


def _row_copy(src_ref, src_row, dst_ref, dst_row, sem):
    return pltpu.make_async_copy(src_ref.at[pl.ds(src_row, 1)], dst_ref.at[pl.ds(dst_row, 1)], sem)


def _dispatch_kernel(p1_ref, p2_ref, u_ref, xs_in_ref, xs_ref, sem):
    del xs_in_ref
    rows = u_ref.shape[0]
    base = pl.program_id(0) * rows

    def issue(r, _):
        _row_copy(u_ref, r, xs_ref, p1_ref[base + r], sem).start()
        _row_copy(u_ref, r, xs_ref, p2_ref[base + r], sem).start()
        return 0

    lax.fori_loop(0, rows, issue, 0, unroll=8)

    def drain(r, _):
        _row_copy(u_ref, 0, xs_ref, 0, sem).wait()
        _row_copy(u_ref, 0, xs_ref, 0, sem).wait()
        return 0

    lax.fori_loop(0, rows, drain, 0, unroll=8)


def _dispatch(u_packed, p1, p2, n_rows):
    t, half = u_packed.shape
    ts = _tile(t, 512)
    grid_spec = pltpu.PrefetchScalarGridSpec(
        num_scalar_prefetch=2,
        grid=(t // ts,),
        in_specs=[
            pl.BlockSpec((ts, half), lambda i, p1, p2: (i, 0)),
            pl.BlockSpec(memory_space=pl.ANY),
        ],
        out_specs=pl.BlockSpec(memory_space=pl.ANY),
        scratch_shapes=[pltpu.SemaphoreType.DMA],
    )
    return pl.pallas_call(
        _dispatch_kernel,
        grid_spec=grid_spec,
        out_shape=jax.ShapeDtypeStruct((n_rows, half), I32),
        input_output_aliases={3: 0},
        compiler_params=_params("arbitrary"),
        name="moe_dispatch",
    )(p1, p2, u_packed, jnp.zeros((n_rows, half), I32))


def _experts_kernel(te_ref, nu_ref, xs_ref, wgu_ref, wd_ref, o_ref, wgu_bf, wd_bf):
    i = pl.program_id(0)
    prev = te_ref[jnp.maximum(i - 1, 0)]
    fresh = (i == 0) | (te_ref[i] != prev)

    @pl.when(fresh & (i < nu_ref[0]))
    def _():
        wgu_bf[...] = wgu_ref[0].astype(BF16)
        wd_bf[...] = wd_ref[0].astype(BF16)

    @pl.when(i < nu_ref[0])
    def _():
        hi, lo = _unpack_bf16_pairs(xs_ref[...])
        half = hi.shape[1]
        f = wd_bf.shape[0]
        gu = (jnp.dot(hi, wgu_bf[:half, :], preferred_element_type=F32)
              + jnp.dot(lo, wgu_bf[half:, :], preferred_element_type=F32))
        gate, up = gu[:, :f], gu[:, f:]
        act = gate * (1.0 / (1.0 + jnp.exp(-gate))) * up
        o_ref[...] = jnp.dot(act.astype(BF16), wd_bf[...], preferred_element_type=F32)

    @pl.when(i >= nu_ref[0])
    def _():
        o_ref[...] = jnp.zeros_like(o_ref)


def _experts(xs, tile_expert, n_used, w_gu, w_down, tm):
    n_rows, half = xs.shape
    _, d, f2 = w_gu.shape
    f = f2 // 2
    n_tiles = n_rows // tm
    last = lambda i, te, nu: jnp.minimum(i, nu[0] - 1)
    grid_spec = pltpu.PrefetchScalarGridSpec(
        num_scalar_prefetch=2,
        grid=(n_tiles,),
        in_specs=[
            pl.BlockSpec((tm, half), lambda i, te, nu: (last(i, te, nu), 0)),
            pl.BlockSpec((1, d, f2), lambda i, te, nu: (te[i], 0, 0)),
            pl.BlockSpec((1, f, d), lambda i, te, nu: (te[i], 0, 0)),
        ],
        out_specs=pl.BlockSpec((tm, d), lambda i, te, nu: (i, 0)),
        scratch_shapes=[pltpu.VMEM((d, f2), BF16), pltpu.VMEM((f, d), BF16)],
    )
    return pl.pallas_call(
        _experts_kernel,
        grid_spec=grid_spec,
        out_shape=jax.ShapeDtypeStruct((n_rows, d), F32),
        compiler_params=_params("arbitrary"),
        name="moe_experts",
    )(tile_expert, n_used, xs, w_gu, w_down)


def _combine_kernel(p1_ref, p2_ref, x_ref, meta_ref, mod_ref, lng_ref, lnb_ref, ys_ref, o_ref, r1, r2, sem):
    rows = x_ref.shape[0]
    base = pl.program_id(0) * rows

    def issue(r, _):
        _row_copy(ys_ref, p1_ref[base + r], r1, r, sem).start()
        _row_copy(ys_ref, p2_ref[base + r], r2, r, sem).start()
        return 0

    lax.fori_loop(0, rows, issue, 0, unroll=8)

    def drain(r, _):
        _row_copy(ys_ref, 0, r1, 0, sem).wait()
        _row_copy(ys_ref, 0, r2, 0, sem).wait()
        return 0

    lax.fori_loop(0, rows, drain, 0, unroll=8)

    meta = meta_ref[...]
    y = meta[:, META_W1:META_W1 + 1] * r1[...] + meta[:, META_W2:META_W2 + 1] * r2[...]
    gate_c = mod_ref[0][5:6]
    o_ref[...] = _layer_norm(ALPHA * x_ref[...] + (1.0 + gate_c) * y, lng_ref[...], lnb_ref[...])


def _combine(x1, meta, mod, lng, lnb, ys, p1, p2, seq):
    t, d = x1.shape
    ts = _tile(seq, 256)
    tiles_per_seq = seq // ts
    vec = lambda i, p1, p2: (0, 0)
    grid_spec = pltpu.PrefetchScalarGridSpec(
        num_scalar_prefetch=2,
        grid=(t // ts,),
        in_specs=[
            pl.BlockSpec((ts, d), lambda i, p1, p2: (i, 0)),
            pl.BlockSpec((ts, LANES), lambda i, p1, p2: (i, 0)),
            pl.BlockSpec((1, 6, d), lambda i, p1, p2: (i // tiles_per_seq, 0, 0)),
            pl.BlockSpec((1, d), vec),
            pl.BlockSpec((1, d), vec),
            pl.BlockSpec(memory_space=pl.ANY),
        ],
        out_specs=pl.BlockSpec((ts, d), lambda i, p1, p2: (i, 0)),
        scratch_shapes=[pltpu.VMEM((ts, d), F32), pltpu.VMEM((ts, d), F32), pltpu.SemaphoreType.DMA],
    )
    return pl.pallas_call(
        _combine_kernel,
        grid_spec=grid_spec,
        out_shape=jax.ShapeDtypeStruct((t, d), F32),
        compiler_params=_params("arbitrary"),
        name="moe_combine",
    )(p1, p2, x1, meta, mod, lng.reshape(1, d), lnb.reshape(1, d), ys)


def _moe_layer(x1, u_packed, meta, counts, mod, lng, lnb, w_gu, w_down, seq):
    t, d = x1.shape
    tm = 256
    n_rows = t * TOP_K + N_EXPERTS * tm
    n_tiles = n_rows // tm

    counts = counts[0, ROUTER_LANE0:ROUTER_LANE0 + N_EXPERTS].astype(I32)
    tiles_per_expert = (counts + tm - 1) // tm
    tile_end = jnp.cumsum(tiles_per_expert)
    row_start = (tile_end - tiles_per_expert) * tm
    n_used = tile_end[-1:]
    tile_ids = jnp.arange(n_tiles, dtype=I32)
    tile_expert = jnp.searchsorted(tile_end, jnp.minimum(tile_ids, n_used[0] - 1), side="right").astype(I32)

    e1 = meta[:, META_E1].astype(I32)
    e2 = meta[:, META_E2].astype(I32)
    p1 = row_start[e1] + meta[:, META_R1].astype(I32)
    p2 = row_start[e2] + meta[:, META_R2].astype(I32)

    xs = _dispatch(u_packed, p1, p2, n_rows)
    ys = _experts(xs, tile_expert, n_used, w_gu, w_down, tm)
    return _combine(x1, meta, mod, lng, lnb, ys, p1, p2, seq)


def kernel(x, c, ada_w, ada_b, ln_g, ln_b, pool_w, pool_scale, fox_w_in, fox_b_f, fox_w_o,
           router_w_group, router_b_group, router_w_expert, router_b_expert, moe_w_gate_up, moe_w_down):
    bsz, seq, d = x.shape
    mods = _adaln(c, ada_w, ada_b)
    n_mixers = 2
    for l in range(DEPTH):
        mod = mods[l]
        router = _router_weights(router_w_group[l], router_b_group[l], router_w_expert[l], router_b_expert[l])
        j = l // n_mixers
        if l % n_mixers == 0:
            x1, u_packed, meta, counts = _pool_layer(x, mod, pool_w[j], pool_scale[j], ln_g[l, 0], ln_b[l, 0],
                                                     router)
            x1 = x1.reshape(bsz * seq, d)
        else:
            x1, u_packed, meta, counts = _fox_layer(x.reshape(bsz * seq, d), mod, fox_w_in[j], fox_b_f[j],
                                                    fox_w_o[j], ln_g[l, 0], ln_b[l, 0], router, bsz, seq)
        x = _moe_layer(x1, u_packed, meta, counts, mod, ln_g[l, 1], ln_b[l, 1], moe_w_gate_up[l],
                       moe_w_down[l], seq).reshape(bsz, seq, d)
    return x
```

```python
import functools

import jax
import jax.numpy as jnp
from jax import lax
from jax.experimental import pallas as pl
from jax.experimental.pallas import tpu as pltpu

F32 = jnp.float32
BF16 = jnp.bfloat16
I32 = jnp.int32
HIGHEST = lax.Precision.HIGHEST

DEPTH = 2
POOL_WINDOWS = (2, 4, 8, 16)
POOL_HALO = 16
HEAD_DIM = 128
N_GROUPS = 8
EXPERTS_PER_GROUP = 8
N_EXPERTS = N_GROUPS * EXPERTS_PER_GROUP
TOP_K = 2
ALPHA = (2 * DEPTH) ** 0.25
LN_EPS = 1e-5
LOG2E = 1.4426950408889634

LANES = 128
V7X_VMEM_BYTES = 64 * 1024 * 1024
VMEM_LIMIT_BYTES = V7X_VMEM_BYTES - 8 * 1024 * 1024

META_E1, META_E2, META_W1, META_W2, META_R1, META_R2 = range(6)
META_ROWS = 8
RANK_BITS = 16
ROUTER_LANE0 = N_GROUPS


def _params(*semantics):
    return pltpu.CompilerParams(dimension_semantics=semantics, vmem_limit_bytes=VMEM_LIMIT_BYTES)


def _tile(n, pref):
    t = min(n, pref)
    while n % t:
        t //= 2
    return t


def _adaln_kernel(c_ref, w_ref, b_ref, o_ref):
    c = c_ref[...]
    c_act = c * (1.0 / (1.0 + jnp.exp(-c)))
    o_ref[0] = jnp.dot(c_act, w_ref[0], precision=HIGHEST, preferred_element_type=F32) + b_ref[0]


def _adaln(c, ada_w, ada_b):
    n_layers, d, n = ada_w.shape
    b = c.shape[0]
    rows = -(-b // 8) * 8
    c_pad = jnp.pad(c, ((0, rows - b), (0, 0)))
    tn = _tile(n, 1024)
    out = pl.pallas_call(
        _adaln_kernel,
        grid=(n_layers, n // tn),
        in_specs=[
            pl.BlockSpec((rows, d), lambda l, j: (0, 0)),
            pl.BlockSpec((1, d, tn), lambda l, j: (l, 0, j)),
            pl.BlockSpec((1, 1, tn), lambda l, j: (l, 0, j)),
        ],
        out_specs=pl.BlockSpec((1, rows, tn), lambda l, j: (l, 0, j)),
        out_shape=jax.ShapeDtypeStruct((n_layers, rows, n), F32),
        compiler_params=_params("arbitrary", "arbitrary"),
        name="adaln_mod",
    )(c_pad, ada_w, ada_b.reshape(n_layers, 1, n))
    return out[:, :b].reshape(n_layers, b, 6, d)


def _layer_norm(h, g, b):
    mu = jnp.mean(h, axis=-1, keepdims=True)
    d = h - mu
    var = jnp.mean(d * d, axis=-1, keepdims=True)
    return d * lax.rsqrt(var + LN_EPS) * g + b


def _pack_bf16_pairs(u_hi):
    half = u_hi.shape[1] // 2
    bits = lax.bitcast_convert_type(u_hi.astype(F32), I32)
    return bits[:, :half] | lax.shift_right_logical(bits[:, half:], 16)


def _unpack_bf16_pairs(words):
    hi = lax.bitcast_convert_type(words & jnp.int32(-65536), F32).astype(BF16)
    lo = lax.bitcast_convert_type(lax.shift_left(words, 16), F32).astype(BF16)
    return hi, lo


def _route(logits, cnt_ref):
    rows = logits.shape[0]
    lane = lax.broadcasted_iota(I32, (rows, LANES), 1).astype(F32)
    neg_inf = jnp.float32(-jnp.inf)
    big = jnp.float32(2 * LANES)

    g_mask = lane < N_GROUPS
    gl = jnp.where(g_mask, logits, neg_inf)
    g_max = jnp.max(gl, axis=-1, keepdims=True)
    g_idx = jnp.min(jnp.where(gl == g_max, lane, big), axis=-1, keepdims=True)
    p_group = 1.0 / jnp.sum(jnp.where(g_mask, jnp.exp(gl - g_max), 0.0), axis=-1, keepdims=True)

    lo = ROUTER_LANE0 + EXPERTS_PER_GROUP * g_idx
    e_mask = (lane >= lo) & (lane < lo + EXPERTS_PER_GROUP)
    el = jnp.where(e_mask, logits, neg_inf)
    v1 = jnp.max(el, axis=-1, keepdims=True)
    i1 = jnp.min(jnp.where(el == v1, lane, big), axis=-1, keepdims=True)
    el2 = jnp.where(lane == i1, neg_inf, el)
    v2 = jnp.max(el2, axis=-1, keepdims=True)
    i2 = jnp.min(jnp.where(el2 == v2, lane, big), axis=-1, keepdims=True)
    t = jnp.exp(v2 - v1)
    w1 = p_group / (1.0 + t)
    w2 = p_group * t / (1.0 + t)

    sel1 = lane == i1
    sel2 = lane == i2
    onehot = jnp.where(sel1 | sel2, 1.0, 0.0)
    r_idx = lax.broadcasted_iota(I32, (rows, rows), 0)
    c_idx = lax.broadcasted_iota(I32, (rows, rows), 1)
    strict_lower = jnp.where(c_idx < r_idx, 1.0, 0.0).astype(BF16)
    before = jnp.dot(strict_lower, onehot.astype(BF16), preferred_element_type=F32) + cnt_ref[...]
    r1 = jnp.sum(jnp.where(sel1, before, 0.0), axis=-1, keepdims=True)
    r2 = jnp.sum(jnp.where(sel2, before, 0.0), axis=-1, keepdims=True)
    cnt_ref[...] += jnp.sum(onehot, axis=0, keepdims=True)

    cols = (i1 - ROUTER_LANE0, i2 - ROUTER_LANE0, w1, w2, r1, r2)
    meta = jnp.zeros((rows, LANES), F32)
    for k, col in enumerate(cols):
        meta = jnp.where(lane == k, col, meta)
    return meta


def _post_mixer(x, y, mod, lng, lnb, wr_hi, wr_lo, br, x1_ref, up_ref, meta_ref, metat_ref, cnt_ref):
    gate_t, shift_c, scale_c = mod[2:3], mod[3:4], mod[4:5]
    x1 = _layer_norm(ALPHA * x + (1.0 + gate_t) * y, lng, lnb)
    x1_ref[...] = x1
    u = x1 * (1.0 + scale_c) + shift_c
    u_hi = u.astype(BF16)
    u_lo = (u - u_hi.astype(F32)).astype(BF16)
    up_ref[...] = _pack_bf16_pairs(u_hi)
    logits = (jnp.dot(u_hi, wr_hi, preferred_element_type=F32)
              + jnp.dot(u_lo, wr_hi, preferred_element_type=F32)
              + jnp.dot(u_hi, wr_lo, preferred_element_type=F32)) + br
    meta = _route(logits, cnt_ref)
    meta_ref[...] = meta
    metat_ref[...] = meta.T[:META_ROWS, :]


def _router_weights(w_rg, b_rg, w_re, b_re):
    d = w_rg.shape[0]
    pad = LANES - N_GROUPS - N_EXPERTS
    w = jnp.concatenate([w_rg, w_re, jnp.zeros((d, pad), F32)], axis=1)
    b = jnp.concatenate([b_rg, b_re, jnp.zeros((pad,), F32)]).reshape(1, LANES)
    w_hi = w.astype(BF16)
    w_lo = (w - w_hi.astype(F32)).astype(BF16)
    return w_hi, w_lo, b


def _pool_kernel(x_ref, halo_ref, mod_ref, wp_ref, ps_ref, lng_ref, lnb_ref, wrh_ref, wrl_ref, br_ref,
                 x1_ref, up_ref, meta_ref, metat_ref, cnt_ref):
    b = pl.program_id(0)
    i = pl.program_id(1)

    @pl.when((b == 0) & (i == 0))
    def _():
        cnt_ref[...] = jnp.zeros_like(cnt_ref)

    mod = mod_ref[0]
    shift_t, scale_t = mod[0:1], mod[1:2]
    x = x_ref[0]
    rows, d = x.shape
    group = d // len(POOL_WINDOWS)
    u = x * (1.0 + scale_t) + shift_t
    halo = halo_ref[0] * (1.0 + scale_t) + shift_t
    halo = jnp.where(i > 0, halo, 0.0)
    pos = i * rows + lax.broadcasted_iota(I32, (rows, 1), 0)

    ys = []
    for g, w in enumerate(POOL_WINDOWS):
        cols = slice(g * group, (g + 1) * group)
        s = jnp.concatenate([halo[:, cols], u[:, cols]], axis=0)
        k = 1
        while k < w:
            s = s + pltpu.roll(s, k, axis=0)
            k *= 2
        inv_count = 1.0 / jnp.minimum(pos + 1, w).astype(F32)
        pooled = s[POOL_HALO:, :] * inv_count - u[:, cols]
        mixed = jnp.dot(pooled.astype(BF16), wp_ref[g], preferred_element_type=F32)
        ys.append(mixed * ps_ref[:, cols])
    y = jnp.concatenate(ys, axis=-1)

    _post_mixer(x, y, mod, lng_ref[...], lnb_ref[...], wrh_ref[...], wrl_ref[...], br_ref[...],
                x1_ref.at[0], up_ref, meta_ref, metat_ref, cnt_ref)


def _pool_layer(x, mod, pool_w, pool_scale, lng, lnb, router):
    bsz, s, d = x.shape
    ts = _tile(s, 256)
    nsb = s // ts
    hb = ts // POOL_HALO
    wr_hi, wr_lo, br = router
    vec = lambda b, i: (0, 0)
    return pl.pallas_call(
        _pool_kernel,
        grid=(bsz, nsb),
        in_specs=[
            pl.BlockSpec((1, ts, d), lambda b, i: (b, i, 0)),
            pl.BlockSpec((1, POOL_HALO, d), lambda b, i: (b, jnp.maximum(i * hb - 1, 0), 0)),
            pl.BlockSpec((1, 6, d), lambda b, i: (b, 0, 0)),
            pl.BlockSpec(pool_w.shape, lambda b, i: (0, 0, 0)),
            pl.BlockSpec((1, d), vec),
            pl.BlockSpec((1, d), vec),
            pl.BlockSpec((1, d), vec),
            pl.BlockSpec((d, LANES), vec),
            pl.BlockSpec((d, LANES), vec),
            pl.BlockSpec((1, LANES), vec),
        ],
        out_specs=[
            pl.BlockSpec((1, ts, d), lambda b, i: (b, i, 0)),
            pl.BlockSpec((ts, d // 2), lambda b, i: (b * nsb + i, 0)),
            pl.BlockSpec((ts, LANES), lambda b, i: (b * nsb + i, 0)),
            pl.BlockSpec((META_ROWS, ts), lambda b, i: (0, b * nsb + i)),
            pl.BlockSpec((1, LANES), vec),
        ],
        out_shape=[
            jax.ShapeDtypeStruct((bsz, s, d), F32),
            jax.ShapeDtypeStruct((bsz * s, d // 2), I32),
            jax.ShapeDtypeStruct((bsz * s, LANES), F32),
            jax.ShapeDtypeStruct((META_ROWS, bsz * s), F32),
            jax.ShapeDtypeStruct((1, LANES), F32),
        ],
        compiler_params=_params("arbitrary", "arbitrary"),
        name="pool_layer",
    )(x, x, mod, pool_w.astype(BF16), pool_scale.reshape(1, d), lng.reshape(1, d), lnb.reshape(1, d),
      wr_hi, wr_lo, br)


def _fox_proj_kernel(x_ref, mod_ref, w_ref, wf_ref, bf_ref, qkv_ref, cum_ref, u_scr, carry_scr, *, tiles_per_seq):
    i = pl.program_id(0)
    j = pl.program_id(1)

    @pl.when((j == 0) & (i % tiles_per_seq == 0))
    def _():
        carry_scr[...] = jnp.zeros_like(carry_scr)

    @pl.when(j == 0)
    def _():
        mod = mod_ref[0]
        u = (x_ref[...] * (1.0 + mod[1:2]) + mod[0:1]).astype(BF16)
        u_scr[...] = u
        f_logit = jnp.dot(u, wf_ref[...], preferred_element_type=F32) + bf_ref[...]
        log_f = jnp.minimum(f_logit, 0.0) - jnp.log1p(jnp.exp(-jnp.abs(f_logit)))
        rows = log_f.shape[0]
        r_idx = lax.broadcasted_iota(I32, (rows, rows), 0)
        c_idx = lax.broadcasted_iota(I32, (rows, rows), 1)
        lower = jnp.where(c_idx <= r_idx, 1.0, 0.0)
        cum = jnp.dot(lower, log_f, precision=HIGHEST, preferred_element_type=F32) + carry_scr[...]
        cum_ref[...] = cum
        carry_scr[...] = cum[rows - 1:rows, :]

    qkv_ref[...] = jnp.dot(u_scr[...], w_ref[...], preferred_element_type=F32).astype(BF16)


def _fox_proj(x2d, mod, w_qkv, w_f, b_f, seq):
    t, d = x2d.shape
    n = w_qkv.shape[1]
    tm = _tile(seq, 512)
    tn = _tile(n, 1536)
    tiles_per_seq = seq // tm
    return pl.pallas_call(
        functools.partial(_fox_proj_kernel, tiles_per_seq=tiles_per_seq),
        grid=(t // tm, n // tn),
        in_specs=[
            pl.BlockSpec((tm, d), lambda i, j: (i, 0)),
            pl.BlockSpec((1, 6, d), lambda i, j: (i // tiles_per_seq, 0, 0)),
            pl.BlockSpec((d, tn), lambda i, j: (0, j)),
            pl.BlockSpec((d, LANES), lambda i, j: (0, 0)),
            pl.BlockSpec((1, LANES), lambda i, j: (0, 0)),
        ],
        out_specs=[
            pl.BlockSpec((tm, tn), lambda i, j: (i, j)),
            pl.BlockSpec((tm, LANES), lambda i, j: (i, 0)),
        ],
        out_shape=[
            jax.ShapeDtypeStruct((t, n), BF16),
            jax.ShapeDtypeStruct((t, LANES), F32),
        ],
        scratch_shapes=[pltpu.VMEM((tm, d), BF16), pltpu.VMEM((1, LANES), F32)],
        compiler_params=_params("arbitrary", "arbitrary"),
        name="fox_proj",
    )(x2d, mod, w_qkv, w_f, b_f)


def _split3(col):
    hi = col.astype(BF16).astype(F32)
    rest = col - hi
    mid = rest.astype(BF16).astype(F32)
    lo = (rest - mid).astype(BF16).astype(F32)
    return hi, mid, lo


def _head_column(block, h):
    lane = lax.broadcasted_iota(I32, block.shape, 1)
    return jnp.sum(jnp.where(lane == h, block, 0.0), axis=-1, keepdims=True) * LOG2E


def _fox_attn_kernel(q_ref, k_ref, v_ref, cq_ref, ck_ref, o_ref,
                     qa_scr, kx_scr, vx_scr, s_a, s_b, rm_a, rm_b, m_scr, acc_scr, *, build_rows):
    h = pl.program_id(1)
    qi = pl.program_id(2)
    tq = q_ref.shape[0]
    tk = vx_scr.shape[0]
    seq = k_ref.shape[0]
    lane = lax.broadcasted_iota(I32, (build_rows, LANES), 1)

    @pl.when(qi == 0)
    def _():
        def build(i, _):
            start = pl.multiple_of(i * build_rows, build_rows)
            hi, mid, lo = _split3(_head_column(ck_ref[pl.ds(start, build_rows), :], h))
            kx = jnp.where(lane < 3, 1.0,
                           jnp.where(lane == 3, -hi, jnp.where(lane == 4, -mid, jnp.where(lane == 5, -lo, 0.0))))
            kx_scr[pl.ds(start, build_rows), :] = kx.astype(BF16)
            return 0

        lax.fori_loop(0, seq // build_rows, build, 0)
        lane_v = lax.broadcasted_iota(I32, (tk, LANES), 1)
        vx_scr[...] = jnp.where(lane_v == 0, 1.0, 0.0).astype(BF16)

    lane_q = lax.broadcasted_iota(I32, (tq, LANES), 1)
    hi, mid, lo = _split3(_head_column(cq_ref[...], h))
    qx = jnp.where(lane_q == 0, hi, jnp.where(lane_q == 1, mid, jnp.where(lane_q == 2, lo,
                                                                            jnp.where(lane_q < 6, 1.0, 0.0))))
    qa_scr[:, :HEAD_DIM] = q_ref[...]
    qa_scr[:, HEAD_DIM:] = qx.astype(BF16)
    m_scr[...] = jnp.full(m_scr.shape, -jnp.inf, F32)
    acc_scr[...] = jnp.zeros(acc_scr.shape, F32)

    def produce(c, s_dst, rm_dst, masked):
        start = pl.multiple_of(c * tk, tk)
        kk = jnp.concatenate([k_ref[pl.ds(start, tk), :], kx_scr[pl.ds(start, tk), :]], axis=1)
        s = lax.dot_general(qa_scr[...], kk, (((1,), (1,)), ((), ())), preferred_element_type=F32)
        if masked:
            q_pos = qi * tq + lax.broadcasted_iota(I32, (tq, tk), 0)
            k_pos = start + lax.broadcasted_iota(I32, (tq, tk), 1)
            s = jnp.where(k_pos <= q_pos, s, -jnp.inf)
        s_dst[...] = s
        rm_dst[...] = jnp.broadcast_to(jnp.max(s, axis=-1, keepdims=True), rm_dst.shape)

    def consume(c, s_src, rm_src):
        start = pl.multiple_of(c * tk, tk)
        vv = jnp.concatenate([v_ref[pl.ds(start, tk), :], vx_scr[...]], axis=1)
        m_old = m_scr[...]
        m_new = jnp.maximum(m_old, rm_src[...])
        alpha = jnp.exp2(m_old - m_new)
        m_scr[...] = m_new
        p = jnp.concatenate(
            [jnp.exp2(s_src[:, j * LANES:(j + 1) * LANES] - m_new).astype(BF16) for j in range(tk // LANES)],
            axis=1)
        pv = jnp.dot(p, vv, preferred_element_type=F32)
        for j in range(2 * HEAD_DIM // LANES):
            blk = slice(j * LANES, (j + 1) * LANES)
            acc_scr[:, blk] = alpha * acc_scr[:, blk] + pv[:, blk]

    produce(0, s_a, rm_a, True)
    produce(1, s_b, rm_b, True)

    def pair(j, masked):
        c = 2 * j
        consume(c, s_a, rm_a)
        produce(c + 2, s_a, rm_a, masked)
        consume(c + 1, s_b, rm_b)
        produce(c + 3, s_b, rm_b, masked)

    def unmasked_pair(j, _):
        pair(j, False)
        return 0

    lax.fori_loop(0, qi - 1, unmasked_pair, 0)

    @pl.when(qi > 0)
    def _():
        pair(qi - 1, True)

    consume(2 * qi, s_a, rm_a)
    consume(2 * qi + 1, s_b, rm_b)

    acc = acc_scr[...]
    o_ref[...] = (acc[:, :HEAD_DIM] / acc[:, HEAD_DIM:HEAD_DIM + 1]).astype(BF16)


def _fox_attn(qkv, cum, bsz, seq, d):
    heads = d // HEAD_DIM
    tq = _tile(seq, 1024)
    tk = tq // 2
    nq = seq // tq
    return pl.pallas_call(
        functools.partial(_fox_attn_kernel, build_rows=tk),
        grid=(bsz, heads, nq),
        in_specs=[
            pl.BlockSpec((tq, HEAD_DIM), lambda b, h, i: (b * nq + i, h)),
            pl.BlockSpec((seq, HEAD_DIM), lambda b, h, i: (b, heads + h)),
            pl.BlockSpec((seq, HEAD_DIM), lambda b, h, i: (b, 2 * heads + h)),
            pl.BlockSpec((tq, LANES), lambda b, h, i: (b * nq + i, 0)),
            pl.BlockSpec((seq, LANES), lambda b, h, i: (b, 0)),
        ],
        out_specs=pl.BlockSpec((tq, HEAD_DIM), lambda b, h, i: (b * nq + i, h)),
        out_shape=jax.ShapeDtypeStruct((bsz * seq, d), BF16),
        scratch_shapes=[
            pltpu.VMEM((tq, 2 * HEAD_DIM), BF16),
            pltpu.VMEM((seq, LANES), BF16),
            pltpu.VMEM((tk, LANES), BF16),
            pltpu.VMEM((tq, tk), F32),
            pltpu.VMEM((tq, tk), F32),
            pltpu.VMEM((tq, LANES), F32),
            pltpu.VMEM((tq, LANES), F32),
            pltpu.VMEM((tq, LANES), F32),
            pltpu.VMEM((tq, 2 * HEAD_DIM), F32),
        ],
        compiler_params=_params("arbitrary", "arbitrary", "arbitrary"),
        name="fox_attn",
    )(qkv, qkv, qkv, cum, cum)


def _fox_out_kernel(a_ref, x_ref, mod_ref, wo_ref, lng_ref, lnb_ref, wrh_ref, wrl_ref, br_ref,
                    x1_ref, up_ref, meta_ref, metat_ref, cnt_ref):
    @pl.when(pl.program_id(0) == 0)
    def _():
        cnt_ref[...] = jnp.zeros_like(cnt_ref)

    y = jnp.dot(a_ref[...], wo_ref[...], preferred_element_type=F32)
    _post_mixer(x_ref[...], y, mod_ref[0], lng_ref[...], lnb_ref[...], wrh_ref[...], wrl_ref[...],
                br_ref[...], x1_ref, up_ref, meta_ref, metat_ref, cnt_ref)


def _fox_out(attn, x2d, mod, w_o, lng, lnb, router, seq):
    t, d = x2d.shape
    tm = _tile(seq, 256)
    tiles_per_seq = seq // tm
    wr_hi, wr_lo, br = router
    vec = lambda i: (0, 0)
    return pl.pallas_call(
        _fox_out_kernel,
        grid=(t // tm,),
        in_specs=[
            pl.BlockSpec((tm, d), lambda i: (i, 0)),
            pl.BlockSpec((tm, d), lambda i: (i, 0)),
            pl.BlockSpec((1, 6, d), lambda i: (i // tiles_per_seq, 0, 0)),
            pl.BlockSpec((d, d), vec),
            pl.BlockSpec((1, d), vec),
            pl.BlockSpec((1, d), vec),
            pl.BlockSpec((d, LANES), vec),
            pl.BlockSpec((d, LANES), vec),
            pl.BlockSpec((1, LANES), vec),
        ],
        out_specs=[
            pl.BlockSpec((tm, d), lambda i: (i, 0)),
            pl.BlockSpec((tm, d // 2), lambda i: (i, 0)),
            pl.BlockSpec((tm, LANES), lambda i: (i, 0)),
            pl.BlockSpec((META_ROWS, tm), lambda i: (0, i)),
            pl.BlockSpec((1, LANES), vec),
        ],
        out_shape=[
            jax.ShapeDtypeStruct((t, d), F32),
            jax.ShapeDtypeStruct((t, d // 2), I32),
            jax.ShapeDtypeStruct((t, LANES), F32),
            jax.ShapeDtypeStruct((META_ROWS, t), F32),
            jax.ShapeDtypeStruct((1, LANES), F32),
        ],
        compiler_params=_params("arbitrary"),
        name="fox_out",
    )(attn, x2d, mod, w_o, lng.reshape(1, d), lnb.reshape(1, d), wr_hi, wr_lo, br)


def _fox_layer(x2d, mod, w_in, b_f, w_o, lng, lnb, router, bsz, seq):
    t, d = x2d.shape
    heads = d // HEAD_DIM
    q_scale = HEAD_DIM ** -0.5 * LOG2E
    w_qkv = jnp.concatenate([w_in[:, :d] * q_scale, w_in[:, d:3 * d]], axis=1).astype(BF16)
    w_f = jnp.pad(w_in[:, 3 * d:], ((0, 0), (0, LANES - heads))).astype(BF16)
    b_f = jnp.pad(b_f, (0, LANES - heads)).reshape(1, LANES)
    qkv, cum = _fox_proj(x2d, mod, w_qkv, w_f, b_f, seq)
    attn = _fox_attn(qkv, cum, bsz, seq, d)
    return _fox_out(attn, x2d, mod, w_o.astype(BF16), lng, lnb, router, seq)


def _row_copy(src_ref, src_row, dst_ref, dst_row, sem):
    return pltpu.make_async_copy(src_ref.at[pl.ds(src_row, 1)], dst_ref.at[pl.ds(dst_row, 1)], sem)


def _sorted_row(code, row_start_ref):
    return row_start_ref[lax.shift_right_logical(code, RANK_BITS)] + (code & ((1 << RANK_BITS) - 1))


def _dispatch_kernel(c1_ref, c2_ref, rs_ref, u_ref, xs_in_ref, xs_ref, sem):
    del xs_in_ref
    rows = u_ref.shape[0]
    base = pl.program_id(0) * rows

    def issue(r, _):
        _row_copy(u_ref, r, xs_ref, _sorted_row(c1_ref[base + r], rs_ref), sem).start()
        _row_copy(u_ref, r, xs_ref, _sorted_row(c2_ref[base + r], rs_ref), sem).start()
        return 0

    lax.fori_loop(0, rows, issue, 0, unroll=8)

    def drain(r, _):
        _row_copy(u_ref, 0, xs_ref, 0, sem).wait()
        _row_copy(u_ref, 0, xs_ref, 0, sem).wait()
        return 0

    lax.fori_loop(0, rows, drain, 0, unroll=8)


def _dispatch(u_packed, code1, code2, row_start, n_rows):
    t, half = u_packed.shape
    ts = _tile(t, 512)
    grid_spec = pltpu.PrefetchScalarGridSpec(
        num_scalar_prefetch=3,
        grid=(t // ts,),
        in_specs=[
            pl.BlockSpec((ts, half), lambda i, c1, c2, rs: (i, 0)),
            pl.BlockSpec(memory_space=pl.ANY),
        ],
        out_specs=pl.BlockSpec(memory_space=pl.ANY),
        scratch_shapes=[pltpu.SemaphoreType.DMA],
    )
    return pl.pallas_call(
        _dispatch_kernel,
        grid_spec=grid_spec,
        out_shape=jax.ShapeDtypeStruct((n_rows, half), I32),
        input_output_aliases={4: 0},
        compiler_params=_params("arbitrary"),
        name="moe_dispatch",
    )(code1, code2, row_start, u_packed, jnp.zeros((n_rows, half), I32))


def _experts_kernel(te_ref, nu_ref, xs_ref, wgu_ref, wd_ref, o_ref, wgu_bf, wd_bf):
    i = pl.program_id(0)
    prev = te_ref[jnp.maximum(i - 1, 0)]
    fresh = (i == 0) | (te_ref[i] != prev)

    @pl.when(fresh & (i < nu_ref[0]))
    def _():
        wgu_bf[...] = wgu_ref[0, 0].astype(BF16)
        wd_bf[...] = wd_ref[0, 0].astype(BF16)

    @pl.when(i < nu_ref[0])
    def _():
        hi, lo = _unpack_bf16_pairs(xs_ref[...])
        half = hi.shape[1]
        f = wd_bf.shape[0]
        gu = (jnp.dot(hi, wgu_bf[:half, :], preferred_element_type=F32)
              + jnp.dot(lo, wgu_bf[half:, :], preferred_element_type=F32))
        gate, up = gu[:, :f], gu[:, f:]
        act = gate * (1.0 / (1.0 + jnp.exp(-gate))) * up
        o_ref[...] = jnp.dot(act.astype(BF16), wd_bf[...], preferred_element_type=F32)

    @pl.when(i >= nu_ref[0])
    def _():
        o_ref[...] = jnp.zeros_like(o_ref)


def _experts(xs, tile_expert, n_used, w_gu, w_down, layer, tm):
    n_rows, half = xs.shape
    _, _, d, f2 = w_gu.shape
    f = f2 // 2
    n_tiles = n_rows // tm
    last = lambda i, te, nu: jnp.minimum(i, nu[0] - 1)
    grid_spec = pltpu.PrefetchScalarGridSpec(
        num_scalar_prefetch=2,
        grid=(n_tiles,),
        in_specs=[
            pl.BlockSpec((tm, half), lambda i, te, nu: (last(i, te, nu), 0)),
            pl.BlockSpec((1, 1, d, f2), lambda i, te, nu: (layer, te[i], 0, 0)),
            pl.BlockSpec((1, 1, f, d), lambda i, te, nu: (layer, te[i], 0, 0)),
        ],
        out_specs=pl.BlockSpec((tm, d), lambda i, te, nu: (i, 0)),
        scratch_shapes=[pltpu.VMEM((d, f2), BF16), pltpu.VMEM((f, d), BF16)],
    )
    return pl.pallas_call(
        _experts_kernel,
        grid_spec=grid_spec,
        out_shape=jax.ShapeDtypeStruct((n_rows, d), F32),
        compiler_params=_params("arbitrary"),
        name="moe_experts",
    )(tile_expert, n_used, xs, w_gu, w_down)


def _combine_kernel(c1_ref, c2_ref, rs_ref, x_ref, meta_ref, mod_ref, lng_ref, lnb_ref, ys_ref, o_ref,
                    r1, r2, sem):
    rows = x_ref.shape[0]
    base = pl.program_id(0) * rows

    def issue(r, _):
        _row_copy(ys_ref, _sorted_row(c1_ref[base + r], rs_ref), r1, r, sem).start()
        _row_copy(ys_ref, _sorted_row(c2_ref[base + r], rs_ref), r2, r, sem).start()
        return 0

    lax.fori_loop(0, rows, issue, 0, unroll=8)

    def drain(r, _):
        _row_copy(ys_ref, 0, r1, 0, sem).wait()
        _row_copy(ys_ref, 0, r2, 0, sem).wait()
        return 0

    lax.fori_loop(0, rows, drain, 0, unroll=8)

    meta = meta_ref[...]
    y = meta[:, META_W1:META_W1 + 1] * r1[...] + meta[:, META_W2:META_W2 + 1] * r2[...]
    gate_c = mod_ref[0][5:6]
    o_ref[...] = _layer_norm(ALPHA * x_ref[...] + (1.0 + gate_c) * y, lng_ref[...], lnb_ref[...])


def _combine(x1, meta, mod, lng, lnb, ys, code1, code2, row_start, seq):
    t, d = x1.shape
    ts = _tile(seq, 256)
    tiles_per_seq = seq // ts
    vec = lambda i, c1, c2, rs: (0, 0)
    grid_spec = pltpu.PrefetchScalarGridSpec(
        num_scalar_prefetch=3,
        grid=(t // ts,),
        in_specs=[
            pl.BlockSpec((ts, d), lambda i, c1, c2, rs: (i, 0)),
            pl.BlockSpec((ts, LANES), lambda i, c1, c2, rs: (i, 0)),
            pl.BlockSpec((1, 6, d), lambda i, c1, c2, rs: (i // tiles_per_seq, 0, 0)),
            pl.BlockSpec((1, d), vec),
            pl.BlockSpec((1, d), vec),
            pl.BlockSpec(memory_space=pl.ANY),
        ],
        out_specs=pl.BlockSpec((ts, d), lambda i, c1, c2, rs: (i, 0)),
        scratch_shapes=[pltpu.VMEM((ts, d), F32), pltpu.VMEM((ts, d), F32), pltpu.SemaphoreType.DMA],
    )
    return pl.pallas_call(
        _combine_kernel,
        grid_spec=grid_spec,
        out_shape=jax.ShapeDtypeStruct((t, d), F32),
        compiler_params=_params("arbitrary"),
        name="moe_combine",
    )(code1, code2, row_start, x1, meta, mod, lng.reshape(1, d), lnb.reshape(1, d), ys)


def _moe_layer(x1, u_packed, meta, meta_t, counts, mod, lng, lnb, w_gu, w_down, layer, seq):
    t, d = x1.shape
    tm = 256
    n_rows = t * TOP_K + N_EXPERTS * tm
    n_tiles = n_rows // tm

    counts = counts[0, ROUTER_LANE0:ROUTER_LANE0 + N_EXPERTS].astype(I32)
    tiles_per_expert = (counts + tm - 1) // tm
    tile_end = jnp.cumsum(tiles_per_expert)
    row_start = (tile_end - tiles_per_expert) * tm
    n_used = tile_end[-1:]
    tile_ids = jnp.minimum(jnp.arange(n_tiles, dtype=I32), n_used[0] - 1)
    tile_expert = jnp.sum((tile_end[None, :] <= tile_ids[:, None]).astype(I32), axis=1)

    ids = meta_t.astype(I32)
    code1 = (ids[META_E1] << RANK_BITS) | ids[META_R1]
    code2 = (ids[META_E2] << RANK_BITS) | ids[META_R2]

    xs = _dispatch(u_packed, code1, code2, row_start, n_rows)
    ys = _experts(xs, tile_expert, n_used, w_gu, w_down, layer, tm)
    return _combine(x1, meta, mod, lng, lnb, ys, code1, code2, row_start, seq)


def kernel(x, c, ada_w, ada_b, ln_g, ln_b, pool_w, pool_scale, fox_w_in, fox_b_f, fox_w_o,
           router_w_group, router_b_group, router_w_expert, router_b_expert, moe_w_gate_up, moe_w_down):
    bsz, seq, d = x.shape
    mods = _adaln(c, ada_w, ada_b)
    n_mixers = 2
    for l in range(DEPTH):
        mod = mods[l]
        router = _router_weights(router_w_group[l], router_b_group[l], router_w_expert[l], router_b_expert[l])
        j = l // n_mixers
        if l % n_mixers == 0:
            x1, u_packed, meta, meta_t, counts = _pool_layer(x, mod, pool_w[j], pool_scale[j], ln_g[l, 0],
                                                             ln_b[l, 0], router)
            x1 = x1.reshape(bsz * seq, d)
        else:
            x1, u_packed, meta, meta_t, counts = _fox_layer(x.reshape(bsz * seq, d), mod, fox_w_in[j], fox_b_f[j],
                                                            fox_w_o[j], ln_g[l, 0], ln_b[l, 0], router, bsz, seq)
        x = _moe_layer(x1, u_packed, meta, meta_t, counts, mod, ln_g[l, 1], ln_b[l, 1], moe_w_gate_up,
                       moe_w_down, l, seq).reshape(bsz, seq, d)
    return x
```

```python
import functools

import jax
import jax.numpy as jnp
from jax import lax
from jax.experimental import pallas as pl
from jax.experimental.pallas import tpu as pltpu

F32 = jnp.float32
BF16 = jnp.bfloat16
I32 = jnp.int32
HIGHEST = lax.Precision.HIGHEST

DEPTH = 2
POOL_WINDOWS = (2, 4, 8, 16)
POOL_HALO = 16
HEAD_DIM = 128
N_GROUPS = 8
EXPERTS_PER_GROUP = 8
N_EXPERTS = N_GROUPS * EXPERTS_PER_GROUP
TOP_K = 2
ALPHA = (2 * DEPTH) ** 0.25
LN_EPS = 1e-5
LOG2E = 1.4426950408889634

LANES = 128
V7X_VMEM_BYTES = 64 * 1024 * 1024
VMEM_LIMIT_BYTES = V7X_VMEM_BYTES - 8 * 1024 * 1024

META_E1, META_E2, META_W1, META_W2, META_R1, META_R2 = range(6)
META_ROWS = 8
RANK_BITS = 16
ROUTER_LANE0 = N_GROUPS


def _params(*semantics):
    return pltpu.CompilerParams(dimension_semantics=semantics, vmem_limit_bytes=VMEM_LIMIT_BYTES)


def _tile(n, pref):
    t = min(n, pref)
    while n % t:
        t //= 2
    return t


def _adaln_kernel(c_ref, w_ref, b_ref, o_ref):
    c = c_ref[...]
    c_act = c * (1.0 / (1.0 + jnp.exp(-c)))
    o_ref[0] = jnp.dot(c_act, w_ref[0], precision=HIGHEST, preferred_element_type=F32) + b_ref[0]


def _adaln(c, ada_w, ada_b):
    n_layers, d, n = ada_w.shape
    b = c.shape[0]
    rows = -(-b // 8) * 8
    c_pad = jnp.pad(c, ((0, rows - b), (0, 0)))
    tn = _tile(n, 1024)
    out = pl.pallas_call(
        _adaln_kernel,
        grid=(n_layers, n // tn),
        in_specs=[
            pl.BlockSpec((rows, d), lambda l, j: (0, 0)),
            pl.BlockSpec((1, d, tn), lambda l, j: (l, 0, j)),
            pl.BlockSpec((1, 1, tn), lambda l, j: (l, 0, j)),
        ],
        out_specs=pl.BlockSpec((1, rows, tn), lambda l, j: (l, 0, j)),
        out_shape=jax.ShapeDtypeStruct((n_layers, rows, n), F32),
        compiler_params=_params("arbitrary", "arbitrary"),
        name="adaln_mod",
    )(c_pad, ada_w, ada_b.reshape(n_layers, 1, n))
    return out[:, :b].reshape(n_layers, b, 6, d)


def _layer_norm(h, g, b):
    mu = jnp.mean(h, axis=-1, keepdims=True)
    d = h - mu
    var = jnp.mean(d * d, axis=-1, keepdims=True)
    return d * lax.rsqrt(var + LN_EPS) * g + b


def _pack_bf16_pairs(u_hi):
    half = u_hi.shape[1] // 2
    bits = lax.bitcast_convert_type(u_hi.astype(F32), I32)
    return bits[:, :half] | lax.shift_right_logical(bits[:, half:], 16)


def _unpack_pairs_f32(words):
    hi = lax.bitcast_convert_type(words & jnp.int32(-65536), F32)
    lo = lax.bitcast_convert_type(lax.shift_left(words, 16), F32)
    return hi, lo


def _unpack_bf16_pairs(words):
    hi, lo = _unpack_pairs_f32(words)
    return hi.astype(BF16), lo.astype(BF16)


def _route(logits, cnt_ref):
    rows = logits.shape[0]
    lane = lax.broadcasted_iota(I32, (rows, LANES), 1).astype(F32)
    neg_inf = jnp.float32(-jnp.inf)
    big = jnp.float32(2 * LANES)

    g_mask = lane < N_GROUPS
    gl = jnp.where(g_mask, logits, neg_inf)
    g_max = jnp.max(gl, axis=-1, keepdims=True)
    g_idx = jnp.min(jnp.where(gl == g_max, lane, big), axis=-1, keepdims=True)
    p_group = 1.0 / jnp.sum(jnp.where(g_mask, jnp.exp(gl - g_max), 0.0), axis=-1, keepdims=True)

    lo = ROUTER_LANE0 + EXPERTS_PER_GROUP * g_idx
    e_mask = (lane >= lo) & (lane < lo + EXPERTS_PER_GROUP)
    el = jnp.where(e_mask, logits, neg_inf)
    v1 = jnp.max(el, axis=-1, keepdims=True)
    i1 = jnp.min(jnp.where(el == v1, lane, big), axis=-1, keepdims=True)
    el2 = jnp.where(lane == i1, neg_inf, el)
    v2 = jnp.max(el2, axis=-1, keepdims=True)
    i2 = jnp.min(jnp.where(el2 == v2, lane, big), axis=-1, keepdims=True)
    t = jnp.exp(v2 - v1)
    w1 = p_group / (1.0 + t)
    w2 = p_group * t / (1.0 + t)

    sel1 = lane == i1
    sel2 = lane == i2
    onehot = jnp.where(sel1 | sel2, 1.0, 0.0)
    r_idx = lax.broadcasted_iota(I32, (rows, rows), 0)
    c_idx = lax.broadcasted_iota(I32, (rows, rows), 1)
    strict_lower = jnp.where(c_idx < r_idx, 1.0, 0.0).astype(BF16)
    before = jnp.dot(strict_lower, onehot.astype(BF16), preferred_element_type=F32) + cnt_ref[...]
    r1 = jnp.sum(jnp.where(sel1, before, 0.0), axis=-1, keepdims=True)
    r2 = jnp.sum(jnp.where(sel2, before, 0.0), axis=-1, keepdims=True)
    cnt_ref[...] += jnp.sum(onehot, axis=0, keepdims=True)

    cols = (i1 - ROUTER_LANE0, i2 - ROUTER_LANE0, w1, w2, r1, r2)
    meta = jnp.zeros((rows, LANES), F32)
    for k, col in enumerate(cols):
        meta = jnp.where(lane == k, col, meta)
    return meta


def _post_mixer(x, y, mod, lng, lnb, wr_hi, wr_lo, br, x1_ref, up_ref, meta_ref, metat_ref, cnt_ref):
    gate_t, shift_c, scale_c = mod[2:3], mod[3:4], mod[4:5]
    x1 = _layer_norm(ALPHA * x + (1.0 + gate_t) * y, lng, lnb)
    x1_ref[...] = x1
    u = x1 * (1.0 + scale_c) + shift_c
    u_hi = u.astype(BF16)
    u_lo = (u - u_hi.astype(F32)).astype(BF16)
    up_ref[...] = _pack_bf16_pairs(u_hi)
    logits = (jnp.dot(u_hi, wr_hi, preferred_element_type=F32)
              + jnp.dot(u_lo, wr_hi, preferred_element_type=F32)
              + jnp.dot(u_hi, wr_lo, preferred_element_type=F32)) + br
    meta = _route(logits, cnt_ref)
    meta_ref[...] = meta
    metat_ref[...] = meta.T[:META_ROWS, :]


def _router_weights(w_rg, b_rg, w_re, b_re):
    d = w_rg.shape[0]
    pad = LANES - N_GROUPS - N_EXPERTS
    w = jnp.concatenate([w_rg, w_re, jnp.zeros((d, pad), F32)], axis=1)
    b = jnp.concatenate([b_rg, b_re, jnp.zeros((pad,), F32)]).reshape(1, LANES)
    w_hi = w.astype(BF16)
    w_lo = (w - w_hi.astype(F32)).astype(BF16)
    return w_hi, w_lo, b


def _pool_kernel(x_ref, halo_ref, mod_ref, wp_ref, ps_ref, lng_ref, lnb_ref, wrh_ref, wrl_ref, br_ref,
                 x1_ref, up_ref, meta_ref, metat_ref, cnt_ref):
    b = pl.program_id(0)
    i = pl.program_id(1)

    @pl.when((b == 0) & (i == 0))
    def _():
        cnt_ref[...] = jnp.zeros_like(cnt_ref)

    mod = mod_ref[0]
    shift_t, scale_t = mod[0:1], mod[1:2]
    x = x_ref[0]
    rows, d = x.shape
    group = d // len(POOL_WINDOWS)
    u = x * (1.0 + scale_t) + shift_t
    halo = halo_ref[0] * (1.0 + scale_t) + shift_t
    halo = jnp.where(i > 0, halo, 0.0)
    pos = i * rows + lax.broadcasted_iota(I32, (rows, 1), 0)

    ys = []
    for g, w in enumerate(POOL_WINDOWS):
        cols = slice(g * group, (g + 1) * group)
        s = jnp.concatenate([halo[:, cols], u[:, cols]], axis=0)
        k = 1
        while k < w:
            s = s + pltpu.roll(s, k, axis=0)
            k *= 2
        inv_count = 1.0 / jnp.minimum(pos + 1, w).astype(F32)
        pooled = s[POOL_HALO:, :] * inv_count - u[:, cols]
        mixed = jnp.dot(pooled.astype(BF16), wp_ref[g], preferred_element_type=F32)
        ys.append(mixed * ps_ref[:, cols])
    y = jnp.concatenate(ys, axis=-1)

    _post_mixer(x, y, mod, lng_ref[...], lnb_ref[...], wrh_ref[...], wrl_ref[...], br_ref[...],
                x1_ref.at[0], up_ref, meta_ref, metat_ref, cnt_ref)


def _pool_layer(x, mod, pool_w, pool_scale, lng, lnb, router):
    bsz, s, d = x.shape
    ts = _tile(s, 256)
    nsb = s // ts
    hb = ts // POOL_HALO
    wr_hi, wr_lo, br = router
    vec = lambda b, i: (0, 0)
    return pl.pallas_call(
        _pool_kernel,
        grid=(bsz, nsb),
        in_specs=[
            pl.BlockSpec((1, ts, d), lambda b, i: (b, i, 0)),
            pl.BlockSpec((1, POOL_HALO, d), lambda b, i: (b, jnp.maximum(i * hb - 1, 0), 0)),
            pl.BlockSpec((1, 6, d), lambda b, i: (b, 0, 0)),
            pl.BlockSpec(pool_w.shape, lambda b, i: (0, 0, 0)),
            pl.BlockSpec((1, d), vec),
            pl.BlockSpec((1, d), vec),
            pl.BlockSpec((1, d), vec),
            pl.BlockSpec((d, LANES), vec),
            pl.BlockSpec((d, LANES), vec),
            pl.BlockSpec((1, LANES), vec),
        ],
        out_specs=[
            pl.BlockSpec((1, ts, d), lambda b, i: (b, i, 0)),
            pl.BlockSpec((ts, d // 2), lambda b, i: (b * nsb + i, 0)),
            pl.BlockSpec((ts, LANES), lambda b, i: (b * nsb + i, 0)),
            pl.BlockSpec((META_ROWS, ts), lambda b, i: (0, b * nsb + i)),
            pl.BlockSpec((1, LANES), vec),
        ],
        out_shape=[
            jax.ShapeDtypeStruct((bsz, s, d), F32),
            jax.ShapeDtypeStruct((bsz * s, d // 2), I32),
            jax.ShapeDtypeStruct((bsz * s, LANES), F32),
            jax.ShapeDtypeStruct((META_ROWS, bsz * s), F32),
            jax.ShapeDtypeStruct((1, LANES), F32),
        ],
        compiler_params=_params("arbitrary", "arbitrary"),
        name="pool_layer",
    )(x, x, mod, pool_w.astype(BF16), pool_scale.reshape(1, d), lng.reshape(1, d), lnb.reshape(1, d),
      wr_hi, wr_lo, br)


def _fox_proj_kernel(x_ref, mod_ref, w_ref, wf_ref, bf_ref, qkv_ref, cum_ref, carry_scr, *, tiles_per_seq):
    j = pl.program_id(0)
    i = pl.program_id(1)
    mod = mod_ref[0]
    u = (x_ref[...] * (1.0 + mod[1:2]) + mod[0:1]).astype(BF16)

    @pl.when((j == 0) & (i % tiles_per_seq == 0))
    def _():
        carry_scr[...] = jnp.zeros_like(carry_scr)

    @pl.when(j == 0)
    def _():
        f_logit = jnp.dot(u, wf_ref[...], preferred_element_type=F32) + bf_ref[...]
        log_f = jnp.minimum(f_logit, 0.0) - jnp.log1p(jnp.exp(-jnp.abs(f_logit)))
        rows = log_f.shape[0]
        r_idx = lax.broadcasted_iota(I32, (rows, rows), 0)
        c_idx = lax.broadcasted_iota(I32, (rows, rows), 1)
        lower = jnp.where(c_idx <= r_idx, 1.0, 0.0).astype(BF16)
        cum = carry_scr[...]
        for piece in _split3(log_f):
            cum = cum + jnp.dot(lower, piece.astype(BF16), preferred_element_type=F32)
        cum_ref[...] = cum
        carry_scr[...] = cum[rows - 1:rows, :]

    qkv_ref[...] = jnp.dot(u, w_ref[...], preferred_element_type=F32).astype(BF16)


def _fox_proj(x2d, mod, w_qkv, w_f, b_f, seq):
    t, d = x2d.shape
    n = w_qkv.shape[1]
    tm = _tile(seq, 512)
    tn = _tile(n, 1536)
    tiles_per_seq = seq // tm
    n_i = t // tm
    return pl.pallas_call(
        functools.partial(_fox_proj_kernel, tiles_per_seq=tiles_per_seq),
        grid=(n // tn, n_i),
        in_specs=[
            pl.BlockSpec((tm, d), lambda j, i: (i, 0)),
            pl.BlockSpec((1, 6, d), lambda j, i: (i // tiles_per_seq, 0, 0)),
            pl.BlockSpec((d, tn), lambda j, i: (0, j)),
            pl.BlockSpec((d, LANES), lambda j, i: (0, 0)),
            pl.BlockSpec((1, LANES), lambda j, i: (0, 0)),
        ],
        out_specs=[
            pl.BlockSpec((tm, tn), lambda j, i: (i, j)),
            pl.BlockSpec((tm, LANES), lambda j, i: (jnp.where(j == 0, i, n_i - 1), 0)),
        ],
        out_shape=[
            jax.ShapeDtypeStruct((t, n), BF16),
            jax.ShapeDtypeStruct((t, LANES), F32),
        ],
        scratch_shapes=[pltpu.VMEM((1, LANES), F32)],
        compiler_params=_params("arbitrary", "arbitrary"),
        name="fox_proj",
    )(x2d, mod, w_qkv, w_f, b_f)


def _split3(col):
    hi = col.astype(BF16).astype(F32)
    rest = col - hi
    mid = rest.astype(BF16).astype(F32)
    lo = (rest - mid).astype(BF16).astype(F32)
    return hi, mid, lo


def _head_column(block, h):
    lane = lax.broadcasted_iota(I32, block.shape, 1)
    return jnp.sum(jnp.where(lane == h, block, 0.0), axis=-1, keepdims=True) * LOG2E


def _fox_attn_kernel(q_ref, k_ref, v_ref, cq_ref, ck_ref, o_ref,
                     qa_scr, kx_scr, vx_scr, s_a, s_b, rm_a, rm_b, m_scr, acc_scr, *, build_rows):
    h = pl.program_id(1)
    qi = pl.program_id(2)
    tq = q_ref.shape[0]
    tk = vx_scr.shape[0]
    seq = k_ref.shape[0]
    lane = lax.broadcasted_iota(I32, (build_rows, LANES), 1)

    @pl.when(qi == 0)
    def _():
        def build(i, _):
            start = pl.multiple_of(i * build_rows, build_rows)
            hi, mid, lo = _split3(_head_column(ck_ref[pl.ds(start, build_rows), :], h))
            kx = jnp.where(lane < 3, 1.0,
                           jnp.where(lane == 3, -hi, jnp.where(lane == 4, -mid, jnp.where(lane == 5, -lo, 0.0))))
            kx_scr[pl.ds(start, build_rows), :] = kx.astype(BF16)
            return 0

        lax.fori_loop(0, seq // build_rows, build, 0)
        lane_v = lax.broadcasted_iota(I32, (tk, LANES), 1)
        vx_scr[...] = jnp.where(lane_v == 0, 1.0, 0.0).astype(BF16)

    lane_q = lax.broadcasted_iota(I32, (tq, LANES), 1)
    hi, mid, lo = _split3(_head_column(cq_ref[...], h))
    qx = jnp.where(lane_q == 0, hi, jnp.where(lane_q == 1, mid, jnp.where(lane_q == 2, lo,
                                                                            jnp.where(lane_q < 6, 1.0, 0.0))))
    qa_scr[:, :HEAD_DIM] = q_ref[...]
    qa_scr[:, HEAD_DIM:] = qx.astype(BF16)
    m_scr[...] = jnp.full(m_scr.shape, -jnp.inf, F32)
    acc_scr[...] = jnp.zeros(acc_scr.shape, F32)

    def produce(c, s_dst, rm_dst, masked):
        start = pl.multiple_of(c * tk, tk)
        kk = jnp.concatenate([k_ref[pl.ds(start, tk), :], kx_scr[pl.ds(start, tk), :]], axis=1)
        s = lax.dot_general(qa_scr[...], kk, (((1,), (1,)), ((), ())), preferred_element_type=F32)
        if masked:
            q_pos = qi * tq + lax.broadcasted_iota(I32, (tq, tk), 0)
            k_pos = start + lax.broadcasted_iota(I32, (tq, tk), 1)
            s = jnp.where(k_pos <= q_pos, s, -jnp.inf)
        s_dst[...] = s
        rm_dst[...] = jnp.broadcast_to(jnp.max(s, axis=-1, keepdims=True), rm_dst.shape)

    def consume(c, s_src, rm_src):
        start = pl.multiple_of(c * tk, tk)
        vv = jnp.concatenate([v_ref[pl.ds(start, tk), :], vx_scr[...]], axis=1)
        m_old = m_scr[...]
        m_new = jnp.maximum(m_old, rm_src[...])
        alpha = jnp.exp2(m_old - m_new)
        m_scr[...] = m_new
        p = jnp.concatenate(
            [jnp.exp2(s_src[:, j * LANES:(j + 1) * LANES] - m_new).astype(BF16) for j in range(tk // LANES)],
            axis=1)
        pv = jnp.dot(p, vv, preferred_element_type=F32)
        for j in range(2 * HEAD_DIM // LANES):
            blk = slice(j * LANES, (j + 1) * LANES)
            acc_scr[:, blk] = alpha * acc_scr[:, blk] + pv[:, blk]

    produce(0, s_a, rm_a, True)
    produce(1, s_b, rm_b, True)

    def pair(j, masked):
        c = 2 * j
        consume(c, s_a, rm_a)
        produce(c + 2, s_a, rm_a, masked)
        consume(c + 1, s_b, rm_b)
        produce(c + 3, s_b, rm_b, masked)

    def unmasked_pair(j, _):
        pair(j, False)
        return 0

    lax.fori_loop(0, qi - 1, unmasked_pair, 0)

    @pl.when(qi > 0)
    def _():
        pair(qi - 1, True)

    consume(2 * qi, s_a, rm_a)
    consume(2 * qi + 1, s_b, rm_b)

    acc = acc_scr[...]
    o_ref[...] = (acc[:, :HEAD_DIM] / acc[:, HEAD_DIM:HEAD_DIM + 1]).astype(BF16)


def _fox_attn(qkv, cum, bsz, seq, d):
    heads = d // HEAD_DIM
    tq = _tile(seq, 1024)
    tk = tq // 2
    nq = seq // tq
    return pl.pallas_call(
        functools.partial(_fox_attn_kernel, build_rows=tk),
        grid=(bsz, heads, nq),
        in_specs=[
            pl.BlockSpec((tq, HEAD_DIM), lambda b, h, i: (b * nq + i, h)),
            pl.BlockSpec((seq, HEAD_DIM), lambda b, h, i: (b, heads + h)),
            pl.BlockSpec((seq, HEAD_DIM), lambda b, h, i: (b, 2 * heads + h)),
            pl.BlockSpec((tq, LANES), lambda b, h, i: (b * nq + i, 0)),
            pl.BlockSpec((seq, LANES), lambda b, h, i: (b, 0)),
        ],
        out_specs=pl.BlockSpec((tq, HEAD_DIM), lambda b, h, i: (b * nq + i, h)),
        out_shape=jax.ShapeDtypeStruct((bsz * seq, d), BF16),
        scratch_shapes=[
            pltpu.VMEM((tq, 2 * HEAD_DIM), BF16),
            pltpu.VMEM((seq, LANES), BF16),
            pltpu.VMEM((tk, LANES), BF16),
            pltpu.VMEM((tq, tk), F32),
            pltpu.VMEM((tq, tk), F32),
            pltpu.VMEM((tq, LANES), F32),
            pltpu.VMEM((tq, LANES), F32),
            pltpu.VMEM((tq, LANES), F32),
            pltpu.VMEM((tq, 2 * HEAD_DIM), F32),
        ],
        compiler_params=_params("arbitrary", "arbitrary", "arbitrary"),
        name="fox_attn",
    )(qkv, qkv, qkv, cum, cum)


def _fox_out_kernel(a_ref, x_ref, mod_ref, wo_ref, lng_ref, lnb_ref, wrh_ref, wrl_ref, br_ref,
                    x1_ref, up_ref, meta_ref, metat_ref, cnt_ref):
    @pl.when(pl.program_id(0) == 0)
    def _():
        cnt_ref[...] = jnp.zeros_like(cnt_ref)

    y = jnp.dot(a_ref[...], wo_ref[...], preferred_element_type=F32)
    _post_mixer(x_ref[...], y, mod_ref[0], lng_ref[...], lnb_ref[...], wrh_ref[...], wrl_ref[...],
                br_ref[...], x1_ref, up_ref, meta_ref, metat_ref, cnt_ref)


def _fox_out(attn, x2d, mod, w_o, lng, lnb, router, seq):
    t, d = x2d.shape
    tm = _tile(seq, 256)
    tiles_per_seq = seq // tm
    wr_hi, wr_lo, br = router
    vec = lambda i: (0, 0)
    return pl.pallas_call(
        _fox_out_kernel,
        grid=(t // tm,),
        in_specs=[
            pl.BlockSpec((tm, d), lambda i: (i, 0)),
            pl.BlockSpec((tm, d), lambda i: (i, 0)),
            pl.BlockSpec((1, 6, d), lambda i: (i // tiles_per_seq, 0, 0)),
            pl.BlockSpec((d, d), vec),
            pl.BlockSpec((1, d), vec),
            pl.BlockSpec((1, d), vec),
            pl.BlockSpec((d, LANES), vec),
            pl.BlockSpec((d, LANES), vec),
            pl.BlockSpec((1, LANES), vec),
        ],
        out_specs=[
            pl.BlockSpec((tm, d), lambda i: (i, 0)),
            pl.BlockSpec((tm, d // 2), lambda i: (i, 0)),
            pl.BlockSpec((tm, LANES), lambda i: (i, 0)),
            pl.BlockSpec((META_ROWS, tm), lambda i: (0, i)),
            pl.BlockSpec((1, LANES), vec),
        ],
        out_shape=[
            jax.ShapeDtypeStruct((t, d), F32),
            jax.ShapeDtypeStruct((t, d // 2), I32),
            jax.ShapeDtypeStruct((t, LANES), F32),
            jax.ShapeDtypeStruct((META_ROWS, t), F32),
            jax.ShapeDtypeStruct((1, LANES), F32),
        ],
        compiler_params=_params("arbitrary"),
        name="fox_out",
    )(attn, x2d, mod, w_o, lng.reshape(1, d), lnb.reshape(1, d), wr_hi, wr_lo, br)


def _fox_layer(x2d, mod, w_in, b_f, w_o, lng, lnb, router, bsz, seq):
    t, d = x2d.shape
    heads = d // HEAD_DIM
    q_scale = HEAD_DIM ** -0.5 * LOG2E
    w_qkv = jnp.concatenate([w_in[:, :d] * q_scale, w_in[:, d:3 * d]], axis=1).astype(BF16)
    w_f = jnp.pad(w_in[:, 3 * d:], ((0, 0), (0, LANES - heads))).astype(BF16)
    b_f = jnp.pad(b_f, (0, LANES - heads)).reshape(1, LANES)
    qkv, cum = _fox_proj(x2d, mod, w_qkv, w_f, b_f, seq)
    attn = _fox_attn(qkv, cum, bsz, seq, d)
    return _fox_out(attn, x2d, mod, w_o.astype(BF16), lng, lnb, router, seq)


def _row_copy(src_ref, src_row, dst_ref, dst_row, sem):
    return pltpu.make_async_copy(src_ref.at[pl.ds(src_row, 1)], dst_ref.at[pl.ds(dst_row, 1)], sem)


def _sorted_row(code, row_start_ref):
    return row_start_ref[lax.shift_right_logical(code, RANK_BITS)] + (code & ((1 << RANK_BITS) - 1))


def _dispatch_kernel(c1_ref, c2_ref, rs_ref, u_ref, xs_in_ref, xs_ref, sem):
    del xs_in_ref
    rows = u_ref.shape[0]
    base = pl.program_id(0) * rows

    def issue(r, _):
        _row_copy(u_ref, r, xs_ref, _sorted_row(c1_ref[base + r], rs_ref), sem).start(priority=0)
        _row_copy(u_ref, r, xs_ref, _sorted_row(c2_ref[base + r], rs_ref), sem).start(priority=1)
        return 0

    lax.fori_loop(0, rows, issue, 0, unroll=8)

    def drain(r, _):
        _row_copy(u_ref, 0, xs_ref, 0, sem).wait()
        _row_copy(u_ref, 0, xs_ref, 0, sem).wait()
        return 0

    lax.fori_loop(0, rows, drain, 0, unroll=8)


def _dispatch(u_packed, code1, code2, row_start, n_rows):
    t, half = u_packed.shape
    ts = _tile(t, 512)
    grid_spec = pltpu.PrefetchScalarGridSpec(
        num_scalar_prefetch=3,
        grid=(t // ts,),
        in_specs=[
            pl.BlockSpec((ts, half), lambda i, c1, c2, rs: (i, 0)),
            pl.BlockSpec(memory_space=pl.ANY),
        ],
        out_specs=pl.BlockSpec(memory_space=pl.ANY),
        scratch_shapes=[pltpu.SemaphoreType.DMA],
    )
    return pl.pallas_call(
        _dispatch_kernel,
        grid_spec=grid_spec,
        out_shape=jax.ShapeDtypeStruct((n_rows, half), I32),
        input_output_aliases={4: 0},
        compiler_params=_params("arbitrary"),
        name="moe_dispatch",
    )(code1, code2, row_start, u_packed, jnp.zeros((n_rows, half), I32))


def _experts_kernel(te_ref, nu_ref, xs_ref, wgu_ref, wd_ref, o_ref, wgu_bf, wd_bf):
    i = pl.program_id(0)
    prev = te_ref[jnp.maximum(i - 1, 0)]
    fresh = (i == 0) | (te_ref[i] != prev)

    @pl.when(fresh & (i < nu_ref[0]))
    def _():
        wgu_bf[...] = wgu_ref[0, 0].astype(BF16)
        wd_bf[...] = wd_ref[0, 0].astype(BF16)

    @pl.when(i < nu_ref[0])
    def _():
        hi, lo = _unpack_bf16_pairs(xs_ref[...])
        half = hi.shape[1]
        f = wd_bf.shape[0]
        gu = (jnp.dot(hi, wgu_bf[:half, :], preferred_element_type=F32)
              + jnp.dot(lo, wgu_bf[half:, :], preferred_element_type=F32))
        gate, up = gu[:, :f], gu[:, f:]
        act = gate * (1.0 / (1.0 + jnp.exp(-gate))) * up
        out = jnp.dot(act.astype(BF16), wd_bf[...], preferred_element_type=F32)
        o_ref[...] = _pack_bf16_pairs(out.astype(BF16))

    @pl.when(i >= nu_ref[0])
    def _():
        o_ref[...] = jnp.zeros_like(o_ref)


def _experts(xs, tile_expert, n_used, w_gu, w_down, layer, tm):
    n_rows, half = xs.shape
    _, _, d, f2 = w_gu.shape
    f = f2 // 2
    n_tiles = n_rows // tm
    last = lambda i, te, nu: jnp.minimum(i, nu[0] - 1)
    grid_spec = pltpu.PrefetchScalarGridSpec(
        num_scalar_prefetch=2,
        grid=(n_tiles,),
        in_specs=[
            pl.BlockSpec((tm, half), lambda i, te, nu: (last(i, te, nu), 0)),
            pl.BlockSpec((1, 1, d, f2), lambda i, te, nu: (layer, te[i], 0, 0)),
            pl.BlockSpec((1, 1, f, d), lambda i, te, nu: (layer, te[i], 0, 0)),
        ],
        out_specs=pl.BlockSpec((tm, half), lambda i, te, nu: (i, 0)),
        scratch_shapes=[pltpu.VMEM((d, f2), BF16), pltpu.VMEM((f, d), BF16)],
    )
    return pl.pallas_call(
        _experts_kernel,
        grid_spec=grid_spec,
        out_shape=jax.ShapeDtypeStruct((n_rows, half), I32),
        compiler_params=_params("arbitrary"),
        name="moe_experts",
    )(tile_expert, n_used, xs, w_gu, w_down)


def _combine_kernel(c1_ref, c2_ref, rs_ref, x_ref, meta_ref, mod_ref, lng_ref, lnb_ref, ys_ref, o_ref,
                    r1, r2, sems):
    i = pl.program_id(0)
    rows = x_ref.shape[0]

    def gather(step, slot):
        base = step * rows

        def issue(r, _):
            _row_copy(ys_ref, _sorted_row(c1_ref[base + r], rs_ref), r1.at[slot], r, sems.at[slot]).start(priority=0)
            _row_copy(ys_ref, _sorted_row(c2_ref[base + r], rs_ref), r2.at[slot], r, sems.at[slot]).start(priority=1)
            return 0

        lax.fori_loop(0, rows, issue, 0, unroll=8)

    @pl.when(i == 0)
    def _():
        gather(0, 0)

    @pl.when(i + 1 < pl.num_programs(0))
    def _():
        gather(i + 1, (i + 1) % 2)

    slot = i % 2

    def drain(r, _):
        _row_copy(ys_ref, 0, r1.at[slot], 0, sems.at[slot]).wait()
        _row_copy(ys_ref, 0, r2.at[slot], 0, sems.at[slot]).wait()
        return 0

    lax.fori_loop(0, rows, drain, 0, unroll=8)

    meta = meta_ref[...]
    w1 = meta[:, META_W1:META_W1 + 1]
    w2 = meta[:, META_W2:META_W2 + 1]
    hi1, lo1 = _unpack_pairs_f32(r1[slot])
    hi2, lo2 = _unpack_pairs_f32(r2[slot])
    y = jnp.concatenate([w1 * hi1 + w2 * hi2, w1 * lo1 + w2 * lo2], axis=1)
    gate_c = mod_ref[0][5:6]
    o_ref[...] = _layer_norm(ALPHA * x_ref[...] + (1.0 + gate_c) * y, lng_ref[...], lnb_ref[...])


def _combine(x1, meta, mod, lng, lnb, ys, code1, code2, row_start, seq):
    t, d = x1.shape
    ts = _tile(seq, 256)
    tiles_per_seq = seq // ts
    vec = lambda i, c1, c2, rs: (0, 0)
    grid_spec = pltpu.PrefetchScalarGridSpec(
        num_scalar_prefetch=3,
        grid=(t // ts,),
        in_specs=[
            pl.BlockSpec((ts, d), lambda i, c1, c2, rs: (i, 0)),
            pl.BlockSpec((ts, LANES), lambda i, c1, c2, rs: (i, 0)),
            pl.BlockSpec((1, 6, d), lambda i, c1, c2, rs: (i // tiles_per_seq, 0, 0)),
            pl.BlockSpec((1, d), vec),
            pl.BlockSpec((1, d), vec),
            pl.BlockSpec(memory_space=pl.ANY),
        ],
        out_specs=pl.BlockSpec((ts, d), lambda i, c1, c2, rs: (i, 0)),
        scratch_shapes=[pltpu.VMEM((2, ts, d // 2), I32), pltpu.VMEM((2, ts, d // 2), I32),
                        pltpu.SemaphoreType.DMA((2,))],
    )
    return pl.pallas_call(
        _combine_kernel,
        grid_spec=grid_spec,
        out_shape=jax.ShapeDtypeStruct((t, d), F32),
        compiler_params=_params("arbitrary"),
        name="moe_combine",
    )(code1, code2, row_start, x1, meta, mod, lng.reshape(1, d), lnb.reshape(1, d), ys)


def _moe_layer(x1, u_packed, meta, meta_t, counts, mod, lng, lnb, w_gu, w_down, layer, seq):
    t, d = x1.shape
    tm = 512
    n_rows = t * TOP_K + N_EXPERTS * tm
    n_tiles = n_rows // tm

    counts = counts[0, ROUTER_LANE0:ROUTER_LANE0 + N_EXPERTS].astype(I32)
    tiles_per_expert = (counts + tm - 1) // tm
    tile_end = jnp.cumsum(tiles_per_expert)
    row_start = (tile_end - tiles_per_expert) * tm
    n_used = tile_end[-1:]
    tile_ids = jnp.minimum(jnp.arange(n_tiles, dtype=I32), n_used[0] - 1)
    tile_expert = jnp.sum((tile_end[None, :] <= tile_ids[:, None]).astype(I32), axis=1)

    ids = meta_t.astype(I32)
    code1 = (ids[META_E1] << RANK_BITS) | ids[META_R1]
    code2 = (ids[META_E2] << RANK_BITS) | ids[META_R2]

    xs = _dispatch(u_packed, code1, code2, row_start, n_rows)
    ys = _experts(xs, tile_expert, n_used, w_gu, w_down, layer, tm)
    return _combine(x1, meta, mod, lng, lnb, ys, code1, code2, row_start, seq)


def kernel(x, c, ada_w, ada_b, ln_g, ln_b, pool_w, pool_scale, fox_w_in, fox_b_f, fox_w_o,
           router_w_group, router_b_group, router_w_expert, router_b_expert, moe_w_gate_up, moe_w_down):
    bsz, seq, d = x.shape
    mods = _adaln(c, ada_w, ada_b)
    n_mixers = 2
    for l in range(DEPTH):
        mod = mods[l]
        router = _router_weights(router_w_group[l], router_b_group[l], router_w_expert[l], router_b_expert[l])
        j = l // n_mixers
        if l % n_mixers == 0:
            x1, u_packed, meta, meta_t, counts = _pool_layer(x, mod, pool_w[j], pool_scale[j], ln_g[l, 0],
                                                             ln_b[l, 0], router)
            x1 = x1.reshape(bsz * seq, d)
        else:
            x1, u_packed, meta, meta_t, counts = _fox_layer(x.reshape(bsz * seq, d), mod, fox_w_in[j], fox_b_f[j],
                                                            fox_w_o[j], ln_g[l, 0], ln_b[l, 0], router, bsz, seq)
        x = _moe_layer(x1, u_packed, meta, meta_t, counts, mod, ln_g[l, 1], ln_b[l, 1], moe_w_gate_up,
                       moe_w_down, l, seq).reshape(bsz, seq, d)
    return x
```

```python
import functools

import jax
import jax.numpy as jnp
from jax import lax
from jax.experimental import pallas as pl
from jax.experimental.pallas import tpu as pltpu

F32 = jnp.float32
BF16 = jnp.bfloat16
I32 = jnp.int32
HIGHEST = lax.Precision.HIGHEST

DEPTH = 2
POOL_WINDOWS = (2, 4, 8, 16)
POOL_HALO = 16
HEAD_DIM = 128
N_GROUPS = 8
EXPERTS_PER_GROUP = 8
N_EXPERTS = N_GROUPS * EXPERTS_PER_GROUP
TOP_K = 2
ALPHA = (2 * DEPTH) ** 0.25
LN_EPS = 1e-5
LOG2E = 1.4426950408889634

LANES = 128
V7X_VMEM_BYTES = 64 * 1024 * 1024
VMEM_LIMIT_BYTES = V7X_VMEM_BYTES - 8 * 1024 * 1024

META_E1, META_E2, META_W1, META_W2, META_R1, META_R2 = range(6)
META_ROWS = 8
ROUTER_LANE0 = N_GROUPS


def _params(*semantics):
    return pltpu.CompilerParams(dimension_semantics=semantics, vmem_limit_bytes=VMEM_LIMIT_BYTES)


def _tile(n, pref):
    t = min(n, pref)
    while n % t:
        t //= 2
    return t


def _adaln_kernel(c_ref, w_ref, b_ref, o_ref):
    c = c_ref[...]
    c_act = c * (1.0 / (1.0 + jnp.exp(-c)))
    o_ref[0] = jnp.dot(c_act, w_ref[0], precision=HIGHEST, preferred_element_type=F32) + b_ref[0]


def _adaln(c, ada_w, ada_b):
    n_layers, d, n = ada_w.shape
    b = c.shape[0]
    rows = -(-b // 8) * 8
    c_pad = jnp.pad(c, ((0, rows - b), (0, 0)))
    tn = _tile(n, 1024)
    out = pl.pallas_call(
        _adaln_kernel,
        grid=(n_layers, n // tn),
        in_specs=[
            pl.BlockSpec((rows, d), lambda l, j: (0, 0)),
            pl.BlockSpec((1, d, tn), lambda l, j: (l, 0, j)),
            pl.BlockSpec((1, 1, tn), lambda l, j: (l, 0, j)),
        ],
        out_specs=pl.BlockSpec((1, rows, tn), lambda l, j: (l, 0, j)),
        out_shape=jax.ShapeDtypeStruct((n_layers, rows, n), F32),
        compiler_params=_params("arbitrary", "arbitrary"),
        name="adaln_mod",
    )(c_pad, ada_w, ada_b.reshape(n_layers, 1, n))
    return out[:, :b].reshape(n_layers, b, 6, d)


def _layer_norm(h, g, b):
    mu = jnp.mean(h, axis=-1, keepdims=True)
    d = h - mu
    var = jnp.mean(d * d, axis=-1, keepdims=True)
    return d * lax.rsqrt(var + LN_EPS) * g + b


def _pack_bf16_pairs(u_hi):
    half = u_hi.shape[1] // 2
    bits = lax.bitcast_convert_type(u_hi.astype(F32), I32)
    return bits[:, :half] | lax.shift_right_logical(bits[:, half:], 16)


def _unpack_pairs_f32(words):
    hi = lax.bitcast_convert_type(words & jnp.int32(-65536), F32)
    lo = lax.bitcast_convert_type(lax.shift_left(words, 16), F32)
    return hi, lo


def _unpack_bf16_pairs(words):
    hi, lo = _unpack_pairs_f32(words)
    return hi.astype(BF16), lo.astype(BF16)


def _route(logits, cnt_ref):
    rows = logits.shape[0]
    lane = lax.broadcasted_iota(I32, (rows, LANES), 1).astype(F32)
    neg_inf = jnp.float32(-jnp.inf)
    big = jnp.float32(2 * LANES)

    g_mask = lane < N_GROUPS
    gl = jnp.where(g_mask, logits, neg_inf)
    g_max = jnp.max(gl, axis=-1, keepdims=True)
    g_idx = jnp.min(jnp.where(gl == g_max, lane, big), axis=-1, keepdims=True)
    p_group = 1.0 / jnp.sum(jnp.where(g_mask, jnp.exp(gl - g_max), 0.0), axis=-1, keepdims=True)

    lo = ROUTER_LANE0 + EXPERTS_PER_GROUP * g_idx
    e_mask = (lane >= lo) & (lane < lo + EXPERTS_PER_GROUP)
    el = jnp.where(e_mask, logits, neg_inf)
    v1 = jnp.max(el, axis=-1, keepdims=True)
    i1 = jnp.min(jnp.where(el == v1, lane, big), axis=-1, keepdims=True)
    el2 = jnp.where(lane == i1, neg_inf, el)
    v2 = jnp.max(el2, axis=-1, keepdims=True)
    i2 = jnp.min(jnp.where(el2 == v2, lane, big), axis=-1, keepdims=True)
    t = jnp.exp(v2 - v1)
    w1 = p_group / (1.0 + t)
    w2 = p_group * t / (1.0 + t)

    sel1 = lane == i1
    sel2 = lane == i2
    onehot = jnp.where(sel1 | sel2, 1.0, 0.0)
    r_idx = lax.broadcasted_iota(I32, (rows, rows), 0)
    c_idx = lax.broadcasted_iota(I32, (rows, rows), 1)
    strict_lower = jnp.where(c_idx < r_idx, 1.0, 0.0).astype(BF16)
    before = jnp.dot(strict_lower, onehot.astype(BF16), preferred_element_type=F32) + cnt_ref[...]
    r1 = jnp.sum(jnp.where(sel1, before, 0.0), axis=-1, keepdims=True)
    r2 = jnp.sum(jnp.where(sel2, before, 0.0), axis=-1, keepdims=True)
    cnt_ref[...] += jnp.sum(onehot, axis=0, keepdims=True)

    cols = (i1 - ROUTER_LANE0, i2 - ROUTER_LANE0, w1, w2, r1, r2)
    meta = jnp.zeros((rows, LANES), F32)
    for k, col in enumerate(cols):
        meta = jnp.where(lane == k, col, meta)
    return meta


def _post_mixer(x, y, mod, lng, lnb, wr_hi, wr_lo, br, x1_ref, up_ref, meta_ref, metat_ref, cnt_ref):
    gate_t, shift_c, scale_c = mod[2:3], mod[3:4], mod[4:5]
    x1 = _layer_norm(ALPHA * x + (1.0 + gate_t) * y, lng, lnb)
    x1_ref[...] = x1
    u = x1 * (1.0 + scale_c) + shift_c
    u_hi = u.astype(BF16)
    u_lo = (u - u_hi.astype(F32)).astype(BF16)
    up_ref[...] = _pack_bf16_pairs(u_hi)
    logits = (jnp.dot(u_hi, wr_hi, preferred_element_type=F32)
              + jnp.dot(u_lo, wr_hi, preferred_element_type=F32)
              + jnp.dot(u_hi, wr_lo, preferred_element_type=F32)) + br
    meta = _route(logits, cnt_ref)
    meta_ref[...] = meta
    metat_ref[...] = meta.T[:META_ROWS, :]


def _router_weights(w_rg, b_rg, w_re, b_re):
    d = w_rg.shape[0]
    pad = LANES - N_GROUPS - N_EXPERTS
    w = jnp.concatenate([w_rg, w_re, jnp.zeros((d, pad), F32)], axis=1)
    b = jnp.concatenate([b_rg, b_re, jnp.zeros((pad,), F32)]).reshape(1, LANES)
    w_hi = w.astype(BF16)
    w_lo = (w - w_hi.astype(F32)).astype(BF16)
    return w_hi, w_lo, b


def _pool_kernel(x_ref, halo_ref, mod_ref, wp_ref, ps_ref, lng_ref, lnb_ref, wrh_ref, wrl_ref, br_ref,
                 x1_ref, up_ref, meta_ref, metat_ref, cnt_ref):
    b = pl.program_id(0)
    i = pl.program_id(1)

    @pl.when((b == 0) & (i == 0))
    def _():
        cnt_ref[...] = jnp.zeros_like(cnt_ref)

    mod = mod_ref[0]
    shift_t, scale_t = mod[0:1], mod[1:2]
    x = x_ref[0]
    rows, d = x.shape
    group = d // len(POOL_WINDOWS)
    u = x * (1.0 + scale_t) + shift_t
    halo = halo_ref[0] * (1.0 + scale_t) + shift_t
    halo = jnp.where(i > 0, halo, 0.0)
    pos = i * rows + lax.broadcasted_iota(I32, (rows, 1), 0)

    ys = []
    for g, w in enumerate(POOL_WINDOWS):
        cols = slice(g * group, (g + 1) * group)
        s = jnp.concatenate([halo[:, cols], u[:, cols]], axis=0)
        k = 1
        while k < w:
            s = s + pltpu.roll(s, k, axis=0)
            k *= 2
        inv_count = 1.0 / jnp.minimum(pos + 1, w).astype(F32)
        pooled = s[POOL_HALO:, :] * inv_count - u[:, cols]
        mixed = jnp.dot(pooled.astype(BF16), wp_ref[g], preferred_element_type=F32)
        ys.append(mixed * ps_ref[:, cols])
    y = jnp.concatenate(ys, axis=-1)

    _post_mixer(x, y, mod, lng_ref[...], lnb_ref[...], wrh_ref[...], wrl_ref[...], br_ref[...],
                x1_ref.at[0], up_ref, meta_ref, metat_ref, cnt_ref)


def _pool_layer(x, mod, pool_w, pool_scale, lng, lnb, router):
    bsz, s, d = x.shape
    ts = _tile(s, 256)
    nsb = s // ts
    hb = ts // POOL_HALO
    wr_hi, wr_lo, br = router
    vec = lambda b, i: (0, 0)
    return pl.pallas_call(
        _pool_kernel,
        grid=(bsz, nsb),
        in_specs=[
            pl.BlockSpec((1, ts, d), lambda b, i: (b, i, 0)),
            pl.BlockSpec((1, POOL_HALO, d), lambda b, i: (b, jnp.maximum(i * hb - 1, 0), 0)),
            pl.BlockSpec((1, 6, d), lambda b, i: (b, 0, 0)),
            pl.BlockSpec(pool_w.shape, lambda b, i: (0, 0, 0)),
            pl.BlockSpec((1, d), vec),
            pl.BlockSpec((1, d), vec),
            pl.BlockSpec((1, d), vec),
            pl.BlockSpec((d, LANES), vec),
            pl.BlockSpec((d, LANES), vec),
            pl.BlockSpec((1, LANES), vec),
        ],
        out_specs=[
            pl.BlockSpec((1, ts, d), lambda b, i: (b, i, 0)),
            pl.BlockSpec((ts, d // 2), lambda b, i: (b * nsb + i, 0)),
            pl.BlockSpec((ts, LANES), lambda b, i: (b * nsb + i, 0)),
            pl.BlockSpec((META_ROWS, ts), lambda b, i: (0, b * nsb + i)),
            pl.BlockSpec((1, LANES), vec),
        ],
        out_shape=[
            jax.ShapeDtypeStruct((bsz, s, d), F32),
            jax.ShapeDtypeStruct((bsz * s, d // 2), I32),
            jax.ShapeDtypeStruct((bsz * s, LANES), F32),
            jax.ShapeDtypeStruct((META_ROWS, bsz * s), F32),
            jax.ShapeDtypeStruct((1, LANES), F32),
        ],
        compiler_params=_params("arbitrary", "arbitrary"),
        name="pool_layer",
    )(x, x, mod, pool_w.astype(BF16), pool_scale.reshape(1, d), lng.reshape(1, d), lnb.reshape(1, d),
      wr_hi, wr_lo, br)


def _fox_proj_kernel(u_ref, w_ref, wf_ref, bf_ref, qkv_ref, cum_ref, carry_scr, *, tiles_per_seq):
    j = pl.program_id(0)
    i = pl.program_id(1)
    u = u_ref[...]

    @pl.when((j == 0) & (i % tiles_per_seq == 0))
    def _():
        carry_scr[...] = jnp.zeros_like(carry_scr)

    @pl.when(j == 0)
    def _():
        f_logit = jnp.dot(u, wf_ref[...], preferred_element_type=F32) + bf_ref[...]
        log_f = jnp.minimum(f_logit, 0.0) - jnp.log1p(jnp.exp(-jnp.abs(f_logit)))
        rows = log_f.shape[0]
        r_idx = lax.broadcasted_iota(I32, (rows, rows), 0)
        c_idx = lax.broadcasted_iota(I32, (rows, rows), 1)
        lower = jnp.where(c_idx <= r_idx, 1.0, 0.0).astype(BF16)
        cum = carry_scr[...]
        for piece in _split3(log_f):
            cum = cum + jnp.dot(lower, piece.astype(BF16), preferred_element_type=F32)
        cum_ref[...] = cum
        carry_scr[...] = cum[rows - 1:rows, :]

    qkv_ref[...] = jnp.dot(u, w_ref[...], preferred_element_type=F32).astype(BF16)


def _fox_proj(u, w_qkv, w_f, b_f, seq):
    t, d = u.shape
    n = w_qkv.shape[1]
    tm = _tile(seq, 512)
    tn = _tile(n, 1536)
    tiles_per_seq = seq // tm
    n_i = t // tm
    return pl.pallas_call(
        functools.partial(_fox_proj_kernel, tiles_per_seq=tiles_per_seq),
        grid=(n // tn, n_i),
        in_specs=[
            pl.BlockSpec((tm, d), lambda j, i: (i, 0)),
            pl.BlockSpec((d, tn), lambda j, i: (0, j)),
            pl.BlockSpec((d, LANES), lambda j, i: (0, 0)),
            pl.BlockSpec((1, LANES), lambda j, i: (0, 0)),
        ],
        out_specs=[
            pl.BlockSpec((tm, tn), lambda j, i: (i, j)),
            pl.BlockSpec((tm, LANES), lambda j, i: (jnp.where(j == 0, i, n_i - 1), 0)),
        ],
        out_shape=[
            jax.ShapeDtypeStruct((t, n), BF16),
            jax.ShapeDtypeStruct((t, LANES), F32),
        ],
        scratch_shapes=[pltpu.VMEM((1, LANES), F32)],
        compiler_params=_params("arbitrary", "arbitrary"),
        name="fox_proj",
    )(u, w_qkv, w_f, b_f)


def _split3(col):
    hi = col.astype(BF16).astype(F32)
    rest = col - hi
    mid = rest.astype(BF16).astype(F32)
    lo = (rest - mid).astype(BF16).astype(F32)
    return hi, mid, lo


def _head_column(block, h):
    lane = lax.broadcasted_iota(I32, block.shape, 1)
    return jnp.sum(jnp.where(lane == h, block, 0.0), axis=-1, keepdims=True) * LOG2E


def _fox_attn_kernel(q_ref, k_ref, v_ref, cq_ref, ck_ref, o_ref,
                     qa_scr, kx_scr, vx_scr, s_a, s_b, rm_a, rm_b, m_scr, acc_scr, *, build_rows):
    h = pl.program_id(1)
    qi = pl.program_id(2)
    tq = q_ref.shape[0]
    tk = vx_scr.shape[0]
    seq = k_ref.shape[0]
    lane = lax.broadcasted_iota(I32, (build_rows, LANES), 1)

    @pl.when(qi == 0)
    def _():
        def build(i, _):
            start = pl.multiple_of(i * build_rows, build_rows)
            hi, mid, lo = _split3(_head_column(ck_ref[pl.ds(start, build_rows), :], h))
            kx = jnp.where(lane < 3, 1.0,
                           jnp.where(lane == 3, -hi, jnp.where(lane == 4, -mid, jnp.where(lane == 5, -lo, 0.0))))
            kx_scr[pl.ds(start, build_rows), :] = kx.astype(BF16)
            return 0

        lax.fori_loop(0, seq // build_rows, build, 0)
        lane_v = lax.broadcasted_iota(I32, (tk, LANES), 1)
        vx_scr[...] = jnp.where(lane_v == 0, 1.0, 0.0).astype(BF16)

    lane_q = lax.broadcasted_iota(I32, (tq, LANES), 1)
    hi, mid, lo = _split3(_head_column(cq_ref[...], h))
    qx = jnp.where(lane_q == 0, hi, jnp.where(lane_q == 1, mid, jnp.where(lane_q == 2, lo,
                                                                            jnp.where(lane_q < 6, 1.0, 0.0))))
    qa_scr[:, :HEAD_DIM] = q_ref[...]
    qa_scr[:, HEAD_DIM:] = qx.astype(BF16)
    m_scr[...] = jnp.full(m_scr.shape, -jnp.inf, F32)
    acc_scr[...] = jnp.zeros(acc_scr.shape, F32)

    def produce(c, s_dst, rm_dst, masked):
        start = pl.multiple_of(c * tk, tk)
        kk = jnp.concatenate([k_ref[pl.ds(start, tk), :], kx_scr[pl.ds(start, tk), :]], axis=1)
        s = lax.dot_general(qa_scr[...], kk, (((1,), (1,)), ((), ())), preferred_element_type=F32)
        if masked:
            q_pos = qi * tq + lax.broadcasted_iota(I32, (tq, tk), 0)
            k_pos = start + lax.broadcasted_iota(I32, (tq, tk), 1)
            s = jnp.where(k_pos <= q_pos, s, -jnp.inf)
        s_dst[...] = s
        rm_dst[...] = jnp.broadcast_to(jnp.max(s, axis=-1, keepdims=True), rm_dst.shape)

    def consume(c, s_src, rm_src):
        start = pl.multiple_of(c * tk, tk)
        vv = jnp.concatenate([v_ref[pl.ds(start, tk), :], vx_scr[...]], axis=1)
        m_old = m_scr[...]
        m_new = jnp.maximum(m_old, rm_src[...])
        alpha = jnp.exp2(m_old - m_new)
        m_scr[...] = m_new
        p = jnp.concatenate(
            [jnp.exp2(s_src[:, j * LANES:(j + 1) * LANES] - m_new).astype(BF16) for j in range(tk // LANES)],
            axis=1)
        pv = jnp.dot(p, vv, preferred_element_type=F32)
        for j in range(2 * HEAD_DIM // LANES):
            blk = slice(j * LANES, (j + 1) * LANES)
            acc_scr[:, blk] = alpha * acc_scr[:, blk] + pv[:, blk]

    produce(0, s_a, rm_a, True)
    produce(1, s_b, rm_b, True)

    def pair(j, masked):
        c = 2 * j
        consume(c, s_a, rm_a)
        produce(c + 2, s_a, rm_a, masked)
        consume(c + 1, s_b, rm_b)
        produce(c + 3, s_b, rm_b, masked)

    def two_unmasked_pairs(jj, _):
        pair(2 * jj, False)
        pair(2 * jj + 1, False)
        return 0

    n_unmasked = jnp.maximum(qi - 1, 0)
    lax.fori_loop(0, n_unmasked // 2, two_unmasked_pairs, 0)

    @pl.when(n_unmasked % 2 == 1)
    def _():
        pair(n_unmasked - 1, False)

    @pl.when(qi > 0)
    def _():
        pair(qi - 1, True)

    consume(2 * qi, s_a, rm_a)
    consume(2 * qi + 1, s_b, rm_b)

    acc = acc_scr[...]
    o_ref[...] = (acc[:, :HEAD_DIM] / acc[:, HEAD_DIM:HEAD_DIM + 1]).astype(BF16)


def _fox_attn(qkv, cum, bsz, seq, d):
    heads = d // HEAD_DIM
    tq = _tile(seq, 1024)
    tk = tq // 2
    nq = seq // tq
    return pl.pallas_call(
        functools.partial(_fox_attn_kernel, build_rows=tk),
        grid=(bsz, heads, nq),
        in_specs=[
            pl.BlockSpec((tq, HEAD_DIM), lambda b, h, i: (b * nq + i, h)),
            pl.BlockSpec((seq, HEAD_DIM), lambda b, h, i: (b, heads + h)),
            pl.BlockSpec((seq, HEAD_DIM), lambda b, h, i: (b, 2 * heads + h)),
            pl.BlockSpec((tq, LANES), lambda b, h, i: (b * nq + i, 0)),
            pl.BlockSpec((seq, LANES), lambda b, h, i: (b, 0)),
        ],
        out_specs=pl.BlockSpec((tq, HEAD_DIM), lambda b, h, i: (b * nq + i, h)),
        out_shape=jax.ShapeDtypeStruct((bsz * seq, d), BF16),
        scratch_shapes=[
            pltpu.VMEM((tq, 2 * HEAD_DIM), BF16),
            pltpu.VMEM((seq, LANES), BF16),
            pltpu.VMEM((tk, LANES), BF16),
            pltpu.VMEM((tq, tk), F32),
            pltpu.VMEM((tq, tk), F32),
            pltpu.VMEM((tq, LANES), F32),
            pltpu.VMEM((tq, LANES), F32),
            pltpu.VMEM((tq, LANES), F32),
            pltpu.VMEM((tq, 2 * HEAD_DIM), F32),
        ],
        compiler_params=_params("arbitrary", "arbitrary", "arbitrary"),
        name="fox_attn",
    )(qkv, qkv, qkv, cum, cum)


def _fox_out_kernel(a_ref, x_ref, mod_ref, wo_ref, lng_ref, lnb_ref, wrh_ref, wrl_ref, br_ref,
                    x1_ref, up_ref, meta_ref, metat_ref, cnt_ref):
    @pl.when(pl.program_id(0) == 0)
    def _():
        cnt_ref[...] = jnp.zeros_like(cnt_ref)

    y = jnp.dot(a_ref[...], wo_ref[...], preferred_element_type=F32)
    _post_mixer(x_ref[...], y, mod_ref[0], lng_ref[...], lnb_ref[...], wrh_ref[...], wrl_ref[...],
                br_ref[...], x1_ref, up_ref, meta_ref, metat_ref, cnt_ref)


def _fox_out(attn, x2d, mod, w_o, lng, lnb, router, seq):
    t, d = x2d.shape
    tm = _tile(seq, 256)
    tiles_per_seq = seq // tm
    wr_hi, wr_lo, br = router
    vec = lambda i: (0, 0)
    return pl.pallas_call(
        _fox_out_kernel,
        grid=(t // tm,),
        in_specs=[
            pl.BlockSpec((tm, d), lambda i: (i, 0)),
            pl.BlockSpec((tm, d), lambda i: (i, 0)),
            pl.BlockSpec((1, 6, d), lambda i: (i // tiles_per_seq, 0, 0)),
            pl.BlockSpec((d, d), vec),
            pl.BlockSpec((1, d), vec),
            pl.BlockSpec((1, d), vec),
            pl.BlockSpec((d, LANES), vec),
            pl.BlockSpec((d, LANES), vec),
            pl.BlockSpec((1, LANES), vec),
        ],
        out_specs=[
            pl.BlockSpec((tm, d), lambda i: (i, 0)),
            pl.BlockSpec((tm, d // 2), lambda i: (i, 0)),
            pl.BlockSpec((tm, LANES), lambda i: (i, 0)),
            pl.BlockSpec((META_ROWS, tm), lambda i: (0, i)),
            pl.BlockSpec((1, LANES), vec),
        ],
        out_shape=[
            jax.ShapeDtypeStruct((t, d), F32),
            jax.ShapeDtypeStruct((t, d // 2), I32),
            jax.ShapeDtypeStruct((t, LANES), F32),
            jax.ShapeDtypeStruct((META_ROWS, t), F32),
            jax.ShapeDtypeStruct((1, LANES), F32),
        ],
        compiler_params=_params("arbitrary"),
        name="fox_out",
    )(attn, x2d, mod, w_o, lng.reshape(1, d), lnb.reshape(1, d), wr_hi, wr_lo, br)


def _fox_layer(x2d, u, mod, w_in, b_f, w_o, lng, lnb, router, bsz, seq):
    t, d = x2d.shape
    heads = d // HEAD_DIM
    q_scale = HEAD_DIM ** -0.5 * LOG2E
    w_qkv = jnp.concatenate([w_in[:, :d] * q_scale, w_in[:, d:3 * d]], axis=1).astype(BF16)
    w_f = jnp.pad(w_in[:, 3 * d:], ((0, 0), (0, LANES - heads))).astype(BF16)
    b_f = jnp.pad(b_f, (0, LANES - heads)).reshape(1, LANES)
    qkv, cum = _fox_proj(u, w_qkv, w_f, b_f, seq)
    attn = _fox_attn(qkv, cum, bsz, seq, d)
    return _fox_out(attn, x2d, mod, w_o.astype(BF16), lng, lnb, router, seq)


def _row_copy(src_ref, src_row, dst_ref, dst_row, sem):
    return pltpu.make_async_copy(src_ref.at[pl.ds(src_row, 1)], dst_ref.at[pl.ds(dst_row, 1)], sem)


def _plan_kernel(mt_ref, rs_ref, p_ref):
    mt = mt_ref[...]
    rs = rs_ref[...]
    expert = lax.broadcasted_iota(I32, (N_EXPERTS, mt.shape[1]), 0).astype(F32)
    for k, (e_row, r_row) in enumerate(((META_E1, META_R1), (META_E2, META_R2))):
        start = jnp.sum(jnp.where(expert == mt[e_row:e_row + 1], rs, 0.0), axis=0, keepdims=True)
        p_ref[k:k + 1, :] = (start + mt[r_row:r_row + 1]).astype(I32)


def _plan(meta_t, row_start):
    t = meta_t.shape[1]
    tn = _tile(t, 4096)
    return pl.pallas_call(
        _plan_kernel,
        grid=(t // tn,),
        in_specs=[
            pl.BlockSpec((META_ROWS, tn), lambda i: (0, i)),
            pl.BlockSpec((N_EXPERTS, 1), lambda i: (0, 0)),
        ],
        out_specs=pl.BlockSpec((TOP_K, tn), lambda i: (0, i)),
        out_shape=jax.ShapeDtypeStruct((TOP_K, t), I32),
        compiler_params=_params("arbitrary"),
        name="moe_plan",
    )(meta_t, row_start.astype(F32).reshape(N_EXPERTS, 1))


def _dispatch_kernel(p1_ref, p2_ref, u_ref, xs_in_ref, xs_ref, sem):
    del xs_in_ref
    rows = u_ref.shape[0]
    base = pl.program_id(0) * rows

    def issue(r, _):
        _row_copy(u_ref, r, xs_ref, p1_ref[base + r], sem).start(priority=0)
        _row_copy(u_ref, r, xs_ref, p2_ref[base + r], sem).start(priority=1)
        return 0

    lax.fori_loop(0, rows, issue, 0, unroll=8)

    def drain(r, _):
        _row_copy(u_ref, 0, xs_ref, 0, sem).wait()
        _row_copy(u_ref, 0, xs_ref, 0, sem).wait()
        return 0

    lax.fori_loop(0, rows, drain, 0, unroll=8)


def _dispatch(u_packed, p1, p2, xs_init):
    t, half = u_packed.shape
    ts = _tile(t, 512)
    grid_spec = pltpu.PrefetchScalarGridSpec(
        num_scalar_prefetch=2,
        grid=(t // ts,),
        in_specs=[
            pl.BlockSpec((ts, half), lambda i, p1, p2: (i, 0)),
            pl.BlockSpec(memory_space=pl.ANY),
        ],
        out_specs=pl.BlockSpec(memory_space=pl.ANY),
        scratch_shapes=[pltpu.SemaphoreType.DMA],
    )
    return pl.pallas_call(
        _dispatch_kernel,
        grid_spec=grid_spec,
        out_shape=jax.ShapeDtypeStruct(xs_init.shape, I32),
        input_output_aliases={3: 0},
        compiler_params=_params("arbitrary"),
        name="moe_dispatch",
    )(p1, p2, u_packed, xs_init)


def _experts_kernel(te_ref, nu_ref, xs_ref, wgu_ref, wd_ref, o_ref, wgu_bf, wd_bf):
    i = pl.program_id(0)
    prev = te_ref[jnp.maximum(i - 1, 0)]
    fresh = (i == 0) | (te_ref[i] != prev)

    @pl.when(fresh & (i < nu_ref[0]))
    def _():
        wgu_bf[...] = wgu_ref[0, 0].astype(BF16)
        wd_bf[...] = wd_ref[0, 0].astype(BF16)

    @pl.when(i < nu_ref[0])
    def _():
        hi, lo = _unpack_bf16_pairs(xs_ref[...])
        half = hi.shape[1]
        f = wd_bf.shape[0]
        gu = (jnp.dot(hi, wgu_bf[:half, :], preferred_element_type=F32)
              + jnp.dot(lo, wgu_bf[half:, :], preferred_element_type=F32))
        gate, up = gu[:, :f], gu[:, f:]
        act = gate * (1.0 / (1.0 + jnp.exp(-gate))) * up
        out = jnp.dot(act.astype(BF16), wd_bf[...], preferred_element_type=F32)
        o_ref[...] = _pack_bf16_pairs(out.astype(BF16))

    @pl.when(i >= nu_ref[0])
    def _():
        o_ref[...] = jnp.zeros_like(o_ref)


def _experts(xs, tile_expert, n_used, w_gu, w_down, layer, tm):
    n_rows, half = xs.shape
    _, _, d, f2 = w_gu.shape
    f = f2 // 2
    n_tiles = n_rows // tm
    last = lambda i, te, nu: jnp.minimum(i, nu[0] - 1)
    grid_spec = pltpu.PrefetchScalarGridSpec(
        num_scalar_prefetch=2,
        grid=(n_tiles,),
        in_specs=[
            pl.BlockSpec((tm, half), lambda i, te, nu: (last(i, te, nu), 0)),
            pl.BlockSpec((1, 1, d, f2), lambda i, te, nu: (layer, te[i], 0, 0)),
            pl.BlockSpec((1, 1, f, d), lambda i, te, nu: (layer, te[i], 0, 0)),
        ],
        out_specs=pl.BlockSpec((tm, half), lambda i, te, nu: (i, 0)),
        scratch_shapes=[pltpu.VMEM((d, f2), BF16), pltpu.VMEM((f, d), BF16)],
    )
    return pl.pallas_call(
        _experts_kernel,
        grid_spec=grid_spec,
        out_shape=jax.ShapeDtypeStruct((n_rows, half), I32),
        compiler_params=_params("arbitrary"),
        name="moe_experts",
    )(tile_expert, n_used, xs, w_gu, w_down)


def _combine_kernel(p1_ref, p2_ref, x_ref, meta_ref, mod_ref, lng_ref, lnb_ref, ys_ref, *rest, emit_next):
    if emit_next:
        nmod_ref, o_ref, un_ref, r1, r2, sems = rest
    else:
        o_ref, r1, r2, sems = rest
    i = pl.program_id(0)
    rows = x_ref.shape[0]

    def gather(step, slot):
        base = step * rows

        def issue(r, _):
            _row_copy(ys_ref, p1_ref[base + r], r1.at[slot], r, sems.at[slot]).start(priority=0)
            _row_copy(ys_ref, p2_ref[base + r], r2.at[slot], r, sems.at[slot]).start(priority=1)
            return 0

        lax.fori_loop(0, rows, issue, 0, unroll=8)

    @pl.when(i == 0)
    def _():
        gather(0, 0)

    @pl.when(i + 1 < pl.num_programs(0))
    def _():
        gather(i + 1, (i + 1) % 2)

    slot = i % 2

    def drain(r, _):
        _row_copy(ys_ref, 0, r1.at[slot], 0, sems.at[slot]).wait()
        _row_copy(ys_ref, 0, r2.at[slot], 0, sems.at[slot]).wait()
        return 0

    lax.fori_loop(0, rows, drain, 0, unroll=8)

    meta = meta_ref[...]
    w1 = meta[:, META_W1:META_W1 + 1]
    w2 = meta[:, META_W2:META_W2 + 1]
    hi1, lo1 = _unpack_pairs_f32(r1[slot])
    hi2, lo2 = _unpack_pairs_f32(r2[slot])
    y = jnp.concatenate([w1 * hi1 + w2 * hi2, w1 * lo1 + w2 * lo2], axis=1)
    gate_c = mod_ref[0][5:6]
    x2 = _layer_norm(ALPHA * x_ref[...] + (1.0 + gate_c) * y, lng_ref[...], lnb_ref[...])
    o_ref[...] = x2
    if emit_next:
        nmod = nmod_ref[0]
        un_ref[...] = (x2 * (1.0 + nmod[1:2]) + nmod[0:1]).astype(BF16)


def _combine(x1, meta, mod, next_mod, lng, lnb, ys, p1, p2, seq):
    t, d = x1.shape
    ts = _tile(seq, 256)
    tiles_per_seq = seq // ts
    emit_next = next_mod is not None
    vec = lambda i, p1, p2: (0, 0)
    per_seq = lambda i, p1, p2: (i // tiles_per_seq, 0, 0)
    rows = lambda i, p1, p2: (i, 0)
    in_specs = [
        pl.BlockSpec((ts, d), rows),
        pl.BlockSpec((ts, LANES), rows),
        pl.BlockSpec((1, 6, d), per_seq),
        pl.BlockSpec((1, d), vec),
        pl.BlockSpec((1, d), vec),
        pl.BlockSpec(memory_space=pl.ANY),
    ]
    args = [x1, meta, mod, lng.reshape(1, d), lnb.reshape(1, d), ys]
    out_specs = [pl.BlockSpec((ts, d), rows)]
    out_shape = [jax.ShapeDtypeStruct((t, d), F32)]
    if emit_next:
        in_specs.append(pl.BlockSpec((1, 6, d), per_seq))
        args.append(next_mod)
        out_specs.append(pl.BlockSpec((ts, d), rows))
        out_shape.append(jax.ShapeDtypeStruct((t, d), BF16))
    grid_spec = pltpu.PrefetchScalarGridSpec(
        num_scalar_prefetch=2,
        grid=(t // ts,),
        in_specs=in_specs,
        out_specs=out_specs,
        scratch_shapes=[pltpu.VMEM((2, ts, d // 2), I32), pltpu.VMEM((2, ts, d // 2), I32),
                        pltpu.SemaphoreType.DMA((2,))],
    )
    outs = pl.pallas_call(
        functools.partial(_combine_kernel, emit_next=emit_next),
        grid_spec=grid_spec,
        out_shape=out_shape,
        compiler_params=_params("arbitrary"),
        name="moe_combine",
    )(p1, p2, *args)
    return (outs[0], outs[1]) if emit_next else (outs[0], None)


def _moe_layer(x1, u_packed, meta, meta_t, counts, mod, next_mod, lng, lnb, w_gu, w_down, layer, seq, xs_init):
    t, d = x1.shape
    tm = 512
    n_rows = t * TOP_K + N_EXPERTS * tm
    n_tiles = n_rows // tm

    counts = counts[0, ROUTER_LANE0:ROUTER_LANE0 + N_EXPERTS].astype(I32)
    tiles_per_expert = (counts + tm - 1) // tm
    tile_end = jnp.cumsum(tiles_per_expert)
    row_start = (tile_end - tiles_per_expert) * tm
    n_used = tile_end[-1:]
    tile_ids = jnp.minimum(jnp.arange(n_tiles, dtype=I32), n_used[0] - 1)
    tile_expert = jnp.sum((tile_end[None, :] <= tile_ids[:, None]).astype(I32), axis=1)

    p = _plan(meta_t, row_start)
    if xs_init is None:
        xs_init = jnp.zeros((n_rows, d // 2), I32)
    xs = _dispatch(u_packed, p[0], p[1], xs_init)
    ys = _experts(xs, tile_expert, n_used, w_gu, w_down, layer, tm)
    x2, u_next = _combine(x1, meta, mod, next_mod, lng, lnb, ys, p[0], p[1], seq)
    return x2, u_next, xs


def kernel(x, c, ada_w, ada_b, ln_g, ln_b, pool_w, pool_scale, fox_w_in, fox_b_f, fox_w_o,
           router_w_group, router_b_group, router_w_expert, router_b_expert, moe_w_gate_up, moe_w_down):
    bsz, seq, d = x.shape
    mods = _adaln(c, ada_w, ada_b)
    n_mixers = 2
    is_fox = lambda l: l % n_mixers == 1
    u_mixer, xs_buf = None, None
    for l in range(DEPTH):
        mod = mods[l]
        router = _router_weights(router_w_group[l], router_b_group[l], router_w_expert[l], router_b_expert[l])
        j = l // n_mixers
        if is_fox(l):
            assert u_mixer is not None, "an attention layer takes its modulated input from the layer before it"
            x1, u_packed, meta, meta_t, counts = _fox_layer(x.reshape(bsz * seq, d), u_mixer, mod, fox_w_in[j],
                                                            fox_b_f[j], fox_w_o[j], ln_g[l, 0], ln_b[l, 0], router,
                                                            bsz, seq)
        else:
            x1, u_packed, meta, meta_t, counts = _pool_layer(x, mod, pool_w[j], pool_scale[j], ln_g[l, 0],
                                                             ln_b[l, 0], router)
            x1 = x1.reshape(bsz * seq, d)
        next_mod = mods[l + 1] if l + 1 < DEPTH and is_fox(l + 1) else None
        x, u_mixer, xs_buf = _moe_layer(x1, u_packed, meta, meta_t, counts, mod, next_mod, ln_g[l, 1], ln_b[l, 1],
                                        moe_w_gate_up, moe_w_down, l, seq, xs_buf)
        x = x.reshape(bsz, seq, d)
    return x
```

```python
import functools

import jax
import jax.numpy as jnp
from jax import lax
from jax.experimental import pallas as pl
from jax.experimental.pallas import tpu as pltpu

F32 = jnp.float32
BF16 = jnp.bfloat16
I32 = jnp.int32
HIGHEST = lax.Precision.HIGHEST

DEPTH = 2
POOL_WINDOWS = (2, 4, 8, 16)
POOL_HALO = 16
HEAD_DIM = 128
N_GROUPS = 8
EXPERTS_PER_GROUP = 8
N_EXPERTS = N_GROUPS * EXPERTS_PER_GROUP
TOP_K = 2
ALPHA = (2 * DEPTH) ** 0.25
LN_EPS = 1e-5
LOG2E = 1.4426950408889634

LANES = 128
V7X_VMEM_BYTES = 64 * 1024 * 1024
VMEM_LIMIT_BYTES = V7X_VMEM_BYTES - 8 * 1024 * 1024

META_E1, META_E2, META_W1, META_W2, META_R1, META_R2 = range(6)
META_ROWS = 8
ROUTER_LANE0 = N_GROUPS


def _params(*semantics):
    return pltpu.CompilerParams(dimension_semantics=semantics, vmem_limit_bytes=VMEM_LIMIT_BYTES)


def _tile(n, pref):
    t = min(n, pref)
    while n % t:
        t //= 2
    return t


def _adaln_kernel(c_ref, w_ref, b_ref, o_ref):
    c = c_ref[...]
    c_act = c * (1.0 / (1.0 + jnp.exp(-c)))
    o_ref[0] = jnp.dot(c_act, w_ref[0], precision=HIGHEST, preferred_element_type=F32) + b_ref[0]


def _adaln(c, ada_w, ada_b):
    n_layers, d, n = ada_w.shape
    b = c.shape[0]
    rows = -(-b // 8) * 8
    c_pad = jnp.pad(c, ((0, rows - b), (0, 0)))
    tn = _tile(n, 1024)
    out = pl.pallas_call(
        _adaln_kernel,
        grid=(n_layers, n // tn),
        in_specs=[
            pl.BlockSpec((rows, d), lambda l, j: (0, 0)),
            pl.BlockSpec((1, d, tn), lambda l, j: (l, 0, j)),
            pl.BlockSpec((1, 1, tn), lambda l, j: (l, 0, j)),
        ],
        out_specs=pl.BlockSpec((1, rows, tn), lambda l, j: (l, 0, j)),
        out_shape=jax.ShapeDtypeStruct((n_layers, rows, n), F32),
        compiler_params=_params("arbitrary", "arbitrary"),
        name="adaln_mod",
    )(c_pad, ada_w, ada_b.reshape(n_layers, 1, n))
    return out[:, :b].reshape(n_layers, b, 6, d)


def _layer_norm(h, g, b):
    mu = jnp.mean(h, axis=-1, keepdims=True)
    d = h - mu
    var = jnp.mean(d * d, axis=-1, keepdims=True)
    return d * lax.rsqrt(var + LN_EPS) * g + b


def _pack_bf16_pairs(u_hi):
    half = u_hi.shape[1] // 2
    bits = lax.bitcast_convert_type(u_hi.astype(F32), I32)
    return bits[:, :half] | lax.shift_right_logical(bits[:, half:], 16)


def _unpack_pairs_f32(words):
    hi = lax.bitcast_convert_type(words & jnp.int32(-65536), F32)
    lo = lax.bitcast_convert_type(lax.shift_left(words, 16), F32)
    return hi, lo


def _unpack_bf16_pairs(words):
    hi, lo = _unpack_pairs_f32(words)
    return hi.astype(BF16), lo.astype(BF16)


def _route(logits, cnt_ref):
    rows = logits.shape[0]
    lane = lax.broadcasted_iota(I32, (rows, LANES), 1).astype(F32)
    neg_inf = jnp.float32(-jnp.inf)
    big = jnp.float32(2 * LANES)

    g_mask = lane < N_GROUPS
    gl = jnp.where(g_mask, logits, neg_inf)
    g_max = jnp.max(gl, axis=-1, keepdims=True)
    g_idx = jnp.min(jnp.where(gl == g_max, lane, big), axis=-1, keepdims=True)
    p_group = 1.0 / jnp.sum(jnp.where(g_mask, jnp.exp(gl - g_max), 0.0), axis=-1, keepdims=True)

    lo = ROUTER_LANE0 + EXPERTS_PER_GROUP * g_idx
    e_mask = (lane >= lo) & (lane < lo + EXPERTS_PER_GROUP)
    el = jnp.where(e_mask, logits, neg_inf)
    v1 = jnp.max(el, axis=-1, keepdims=True)
    i1 = jnp.min(jnp.where(el == v1, lane, big), axis=-1, keepdims=True)
    el2 = jnp.where(lane == i1, neg_inf, el)
    v2 = jnp.max(el2, axis=-1, keepdims=True)
    i2 = jnp.min(jnp.where(el2 == v2, lane, big), axis=-1, keepdims=True)
    t = jnp.exp(v2 - v1)
    w1 = p_group / (1.0 + t)
    w2 = p_group * t / (1.0 + t)

    sel1 = lane == i1
    sel2 = lane == i2
    onehot = jnp.where(sel1 | sel2, 1.0, 0.0)
    r_idx = lax.broadcasted_iota(I32, (rows, rows), 0)
    c_idx = lax.broadcasted_iota(I32, (rows, rows), 1)
    strict_lower = jnp.where(c_idx < r_idx, 1.0, 0.0).astype(BF16)
    before = jnp.dot(strict_lower, onehot.astype(BF16), preferred_element_type=F32) + cnt_ref[...]
    r1 = jnp.sum(jnp.where(sel1, before, 0.0), axis=-1, keepdims=True)
    r2 = jnp.sum(jnp.where(sel2, before, 0.0), axis=-1, keepdims=True)
    cnt_ref[...] += jnp.sum(onehot, axis=0, keepdims=True)

    cols = (i1 - ROUTER_LANE0, i2 - ROUTER_LANE0, w1, w2, r1, r2)
    meta = jnp.zeros((rows, LANES), F32)
    for k, col in enumerate(cols):
        meta = jnp.where(lane == k, col, meta)
    return meta


def _post_mixer(x, y, mod, lng, lnb, wr_hi, wr_lo, br, x1_ref, up_ref, meta_ref, metat_ref, cnt_ref):
    gate_t, shift_c, scale_c = mod[2:3], mod[3:4], mod[4:5]
    x1 = _layer_norm(ALPHA * x + (1.0 + gate_t) * y, lng, lnb)
    x1_ref[...] = x1
    u = x1 * (1.0 + scale_c) + shift_c
    u_hi = u.astype(BF16)
    u_lo = (u - u_hi.astype(F32)).astype(BF16)
    up_ref[...] = _pack_bf16_pairs(u_hi)
    logits = (jnp.dot(u_hi, wr_hi, preferred_element_type=F32)
              + jnp.dot(u_lo, wr_hi, preferred_element_type=F32)
              + jnp.dot(u_hi, wr_lo, preferred_element_type=F32)) + br
    meta = _route(logits, cnt_ref)
    meta_ref[...] = meta
    metat_ref[...] = meta.T[:META_ROWS, :]


def _router_weights(w_rg, b_rg, w_re, b_re):
    d = w_rg.shape[0]
    pad = LANES - N_GROUPS - N_EXPERTS
    w = jnp.concatenate([w_rg, w_re, jnp.zeros((d, pad), F32)], axis=1)
    b = jnp.concatenate([b_rg, b_re, jnp.zeros((pad,), F32)]).reshape(1, LANES)
    w_hi = w.astype(BF16)
    w_lo = (w - w_hi.astype(F32)).astype(BF16)
    return w_hi, w_lo, b


def _pool_kernel(x_ref, halo_ref, mod_ref, wp_ref, ps_ref, lng_ref, lnb_ref, wrh_ref, wrl_ref, br_ref,
                 x1_ref, up_ref, meta_ref, metat_ref, cnt_ref):
    b = pl.program_id(0)
    i = pl.program_id(1)

    @pl.when((b == 0) & (i == 0))
    def _():
        cnt_ref[...] = jnp.zeros_like(cnt_ref)

    mod = mod_ref[0]
    shift_t, scale_t = mod[0:1], mod[1:2]
    x = x_ref[0]
    rows, d = x.shape
    group = d // len(POOL_WINDOWS)
    u = x * (1.0 + scale_t) + shift_t
    halo = halo_ref[0] * (1.0 + scale_t) + shift_t
    halo = jnp.where(i > 0, halo, 0.0)
    pos = i * rows + lax.broadcasted_iota(I32, (rows, 1), 0)

    ys = []
    for g, w in enumerate(POOL_WINDOWS):
        cols = slice(g * group, (g + 1) * group)
        s = jnp.concatenate([halo[:, cols], u[:, cols]], axis=0)
        k = 1
        while k < w:
            s = s + pltpu.roll(s, k, axis=0)
            k *= 2
        inv_count = 1.0 / jnp.minimum(pos + 1, w).astype(F32)
        pooled = s[POOL_HALO:, :] * inv_count - u[:, cols]
        mixed = jnp.dot(pooled.astype(BF16), wp_ref[g], preferred_element_type=F32)
        ys.append(mixed * ps_ref[:, cols])
    y = jnp.concatenate(ys, axis=-1)

    _post_mixer(x, y, mod, lng_ref[...], lnb_ref[...], wrh_ref[...], wrl_ref[...], br_ref[...],
                x1_ref.at[0], up_ref, meta_ref, metat_ref, cnt_ref)


def _pool_layer(x, mod, pool_w, pool_scale, lng, lnb, router):
    bsz, s, d = x.shape
    ts = _tile(s, 256)
    nsb = s // ts
    hb = ts // POOL_HALO
    wr_hi, wr_lo, br = router
    vec = lambda b, i: (0, 0)
    return pl.pallas_call(
        _pool_kernel,
        grid=(bsz, nsb),
        in_specs=[
            pl.BlockSpec((1, ts, d), lambda b, i: (b, i, 0)),
            pl.BlockSpec((1, POOL_HALO, d), lambda b, i: (b, jnp.maximum(i * hb - 1, 0), 0)),
            pl.BlockSpec((1, 6, d), lambda b, i: (b, 0, 0)),
            pl.BlockSpec(pool_w.shape, lambda b, i: (0, 0, 0)),
            pl.BlockSpec((1, d), vec),
            pl.BlockSpec((1, d), vec),
            pl.BlockSpec((1, d), vec),
            pl.BlockSpec((d, LANES), vec),
            pl.BlockSpec((d, LANES), vec),
            pl.BlockSpec((1, LANES), vec),
        ],
        out_specs=[
            pl.BlockSpec((1, ts, d), lambda b, i: (b, i, 0)),
            pl.BlockSpec((ts, d // 2), lambda b, i: (b * nsb + i, 0)),
            pl.BlockSpec((ts, LANES), lambda b, i: (b * nsb + i, 0)),
            pl.BlockSpec((META_ROWS, ts), lambda b, i: (0, b * nsb + i)),
            pl.BlockSpec((1, LANES), vec),
        ],
        out_shape=[
            jax.ShapeDtypeStruct((bsz, s, d), F32),
            jax.ShapeDtypeStruct((bsz * s, d // 2), I32),
            jax.ShapeDtypeStruct((bsz * s, LANES), F32),
            jax.ShapeDtypeStruct((META_ROWS, bsz * s), F32),
            jax.ShapeDtypeStruct((1, LANES), F32),
        ],
        compiler_params=_params("arbitrary", "arbitrary"),
        name="pool_layer",
    )(x, x, mod, pool_w.astype(BF16), pool_scale.reshape(1, d), lng.reshape(1, d), lnb.reshape(1, d),
      wr_hi, wr_lo, br)


def _fox_proj_kernel(u_ref, w_ref, wf_ref, bf_ref, qkv_ref, cum_ref, carry_scr, *, tiles_per_seq):
    j = pl.program_id(0)
    i = pl.program_id(1)
    u = u_ref[...]

    @pl.when((j == 0) & (i % tiles_per_seq == 0))
    def _():
        carry_scr[...] = jnp.zeros_like(carry_scr)

    @pl.when(j == 0)
    def _():
        f_logit = jnp.dot(u, wf_ref[...], preferred_element_type=F32) + bf_ref[...]
        log_f = jnp.minimum(f_logit, 0.0) - jnp.log1p(jnp.exp(-jnp.abs(f_logit)))
        rows = log_f.shape[0]
        r_idx = lax.broadcasted_iota(I32, (rows, rows), 0)
        c_idx = lax.broadcasted_iota(I32, (rows, rows), 1)
        lower = jnp.where(c_idx <= r_idx, 1.0, 0.0).astype(BF16)
        cum = carry_scr[...]
        for piece in _split3(log_f):
            cum = cum + jnp.dot(lower, piece.astype(BF16), preferred_element_type=F32)
        cum_ref[...] = cum
        carry_scr[...] = cum[rows - 1:rows, :]

    qkv_ref[...] = jnp.dot(u, w_ref[...], preferred_element_type=F32).astype(BF16)


def _fox_proj(u, w_qkv, w_f, b_f, seq):
    t, d = u.shape
    n = w_qkv.shape[1]
    tm = _tile(seq, 512)
    tn = _tile(n, 1536)
    tiles_per_seq = seq // tm
    n_i = t // tm
    return pl.pallas_call(
        functools.partial(_fox_proj_kernel, tiles_per_seq=tiles_per_seq),
        grid=(n // tn, n_i),
        in_specs=[
            pl.BlockSpec((tm, d), lambda j, i: (i, 0)),
            pl.BlockSpec((d, tn), lambda j, i: (0, j)),
            pl.BlockSpec((d, LANES), lambda j, i: (0, 0)),
            pl.BlockSpec((1, LANES), lambda j, i: (0, 0)),
        ],
        out_specs=[
            pl.BlockSpec((tm, tn), lambda j, i: (i, j)),
            pl.BlockSpec((tm, LANES), lambda j, i: (jnp.where(j == 0, i, n_i - 1), 0)),
        ],
        out_shape=[
            jax.ShapeDtypeStruct((t, n), BF16),
            jax.ShapeDtypeStruct((t, LANES), F32),
        ],
        scratch_shapes=[pltpu.VMEM((1, LANES), F32)],
        compiler_params=_params("arbitrary", "arbitrary"),
        name="fox_proj",
    )(u, w_qkv, w_f, b_f)


def _split3(col):
    hi = col.astype(BF16).astype(F32)
    rest = col - hi
    mid = rest.astype(BF16).astype(F32)
    lo = (rest - mid).astype(BF16).astype(F32)
    return hi, mid, lo


def _head_column(block, h):
    lane = lax.broadcasted_iota(I32, block.shape, 1)
    return jnp.sum(jnp.where(lane == h, block, 0.0), axis=-1, keepdims=True) * LOG2E


def _fox_attn_kernel(q_ref, k_ref, v_ref, cq_ref, ck_ref, o_ref,
                     qa_scr, kx_scr, vx_scr, s_a, s_b, rm_a, rm_b, m_scr, acc_scr, *, build_rows):
    h = pl.program_id(1)
    qi = pl.program_id(2)
    tq = q_ref.shape[0]
    tk = vx_scr.shape[0]
    seq = k_ref.shape[0]
    lane = lax.broadcasted_iota(I32, (build_rows, LANES), 1)

    @pl.when(qi == 0)
    def _():
        def build(i, _):
            start = pl.multiple_of(i * build_rows, build_rows)
            hi, mid, lo = _split3(_head_column(ck_ref[pl.ds(start, build_rows), :], h))
            kx = jnp.where(lane < 3, 1.0,
                           jnp.where(lane == 3, -hi, jnp.where(lane == 4, -mid, jnp.where(lane == 5, -lo, 0.0))))
            kx_scr[pl.ds(start, build_rows), :] = kx.astype(BF16)
            return 0

        lax.fori_loop(0, seq // build_rows, build, 0)
        lane_v = lax.broadcasted_iota(I32, (tk, LANES), 1)
        vx_scr[...] = jnp.where(lane_v == 0, 1.0, 0.0).astype(BF16)

    lane_q = lax.broadcasted_iota(I32, (tq, LANES), 1)
    hi, mid, lo = _split3(_head_column(cq_ref[...], h))
    qx = jnp.where(lane_q == 0, hi, jnp.where(lane_q == 1, mid, jnp.where(lane_q == 2, lo,
                                                                            jnp.where(lane_q < 6, 1.0, 0.0))))
    qa_scr[:, :HEAD_DIM] = q_ref[...]
    qa_scr[:, HEAD_DIM:] = qx.astype(BF16)
    m_scr[...] = jnp.full(m_scr.shape, -jnp.inf, F32)
    acc_scr[...] = jnp.zeros(acc_scr.shape, F32)

    def produce(c, s_dst, rm_dst, masked):
        start = pl.multiple_of(c * tk, tk)
        kk = jnp.concatenate([k_ref[pl.ds(start, tk), :], kx_scr[pl.ds(start, tk), :]], axis=1)
        s = lax.dot_general(qa_scr[...], kk, (((1,), (1,)), ((), ())), preferred_element_type=F32)
        if masked:
            q_pos = qi * tq + lax.broadcasted_iota(I32, (tq, tk), 0)
            k_pos = start + lax.broadcasted_iota(I32, (tq, tk), 1)
            s = jnp.where(k_pos <= q_pos, s, -jnp.inf)
        s_dst[...] = s
        rm_dst[...] = jnp.broadcast_to(jnp.max(s, axis=-1, keepdims=True), rm_dst.shape)

    def consume(c, s_src, rm_src):
        start = pl.multiple_of(c * tk, tk)
        vv = jnp.concatenate([v_ref[pl.ds(start, tk), :], vx_scr[...]], axis=1)
        m_old = m_scr[...]
        m_new = jnp.maximum(m_old, rm_src[...])
        alpha = jnp.exp2(m_old - m_new)
        m_scr[...] = m_new
        p = jnp.concatenate(
            [jnp.exp2(s_src[:, j * LANES:(j + 1) * LANES] - m_new).astype(BF16) for j in range(tk // LANES)],
            axis=1)
        pv = jnp.dot(p, vv, preferred_element_type=F32)
        for j in range(2 * HEAD_DIM // LANES):
            blk = slice(j * LANES, (j + 1) * LANES)
            acc_scr[:, blk] = alpha * acc_scr[:, blk] + pv[:, blk]

    produce(0, s_a, rm_a, True)
    produce(1, s_b, rm_b, True)

    def pair(j, masked):
        c = 2 * j
        consume(c, s_a, rm_a)
        produce(c + 2, s_a, rm_a, masked)
        consume(c + 1, s_b, rm_b)
        produce(c + 3, s_b, rm_b, masked)

    def two_unmasked_pairs(jj, _):
        pair(2 * jj, False)
        pair(2 * jj + 1, False)
        return 0

    n_unmasked = jnp.maximum(qi - 1, 0)
    lax.fori_loop(0, n_unmasked // 2, two_unmasked_pairs, 0)

    @pl.when(n_unmasked % 2 == 1)
    def _():
        pair(n_unmasked - 1, False)

    @pl.when(qi > 0)
    def _():
        pair(qi - 1, True)

    consume(2 * qi, s_a, rm_a)
    consume(2 * qi + 1, s_b, rm_b)

    acc = acc_scr[...]
    o_ref[...] = (acc[:, :HEAD_DIM] / acc[:, HEAD_DIM:HEAD_DIM + 1]).astype(BF16)


def _fox_attn(qkv, cum, bsz, seq, d):
    heads = d // HEAD_DIM
    tq = _tile(seq, 1024)
    tk = tq // 2
    nq = seq // tq
    return pl.pallas_call(
        functools.partial(_fox_attn_kernel, build_rows=tk),
        grid=(bsz, heads, nq),
        in_specs=[
            pl.BlockSpec((tq, HEAD_DIM), lambda b, h, i: (b * nq + i, h)),
            pl.BlockSpec((seq, HEAD_DIM), lambda b, h, i: (b, heads + h)),
            pl.BlockSpec((seq, HEAD_DIM), lambda b, h, i: (b, 2 * heads + h)),
            pl.BlockSpec((tq, LANES), lambda b, h, i: (b * nq + i, 0)),
            pl.BlockSpec((seq, LANES), lambda b, h, i: (b, 0)),
        ],
        out_specs=pl.BlockSpec((tq, HEAD_DIM), lambda b, h, i: (b * nq + i, h)),
        out_shape=jax.ShapeDtypeStruct((bsz * seq, d), BF16),
        scratch_shapes=[
            pltpu.VMEM((tq, 2 * HEAD_DIM), BF16),
            pltpu.VMEM((seq, LANES), BF16),
            pltpu.VMEM((tk, LANES), BF16),
            pltpu.VMEM((tq, tk), F32),
            pltpu.VMEM((tq, tk), F32),
            pltpu.VMEM((tq, LANES), F32),
            pltpu.VMEM((tq, LANES), F32),
            pltpu.VMEM((tq, LANES), F32),
            pltpu.VMEM((tq, 2 * HEAD_DIM), F32),
        ],
        compiler_params=_params("arbitrary", "arbitrary", "arbitrary"),
        name="fox_attn",
    )(qkv, qkv, qkv, cum, cum)


def _fox_out_kernel(a_ref, x_ref, mod_ref, wo_ref, lng_ref, lnb_ref, wrh_ref, wrl_ref, br_ref,
                    x1_ref, up_ref, meta_ref, metat_ref, cnt_ref):
    @pl.when(pl.program_id(0) == 0)
    def _():
        cnt_ref[...] = jnp.zeros_like(cnt_ref)

    y = jnp.dot(a_ref[...], wo_ref[...], preferred_element_type=F32)
    _post_mixer(x_ref[...], y, mod_ref[0], lng_ref[...], lnb_ref[...], wrh_ref[...], wrl_ref[...],
                br_ref[...], x1_ref, up_ref, meta_ref, metat_ref, cnt_ref)


def _fox_out(attn, x2d, mod, w_o, lng, lnb, router, seq):
    t, d = x2d.shape
    tm = _tile(seq, 256)
    tiles_per_seq = seq // tm
    wr_hi, wr_lo, br = router
    vec = lambda i: (0, 0)
    return pl.pallas_call(
        _fox_out_kernel,
        grid=(t // tm,),
        in_specs=[
            pl.BlockSpec((tm, d), lambda i: (i, 0)),
            pl.BlockSpec((tm, d), lambda i: (i, 0)),
            pl.BlockSpec((1, 6, d), lambda i: (i // tiles_per_seq, 0, 0)),
            pl.BlockSpec((d, d), vec),
            pl.BlockSpec((1, d), vec),
            pl.BlockSpec((1, d), vec),
            pl.BlockSpec((d, LANES), vec),
            pl.BlockSpec((d, LANES), vec),
            pl.BlockSpec((1, LANES), vec),
        ],
        out_specs=[
            pl.BlockSpec((tm, d), lambda i: (i, 0)),
            pl.BlockSpec((tm, d // 2), lambda i: (i, 0)),
            pl.BlockSpec((tm, LANES), lambda i: (i, 0)),
            pl.BlockSpec((META_ROWS, tm), lambda i: (0, i)),
            pl.BlockSpec((1, LANES), vec),
        ],
        out_shape=[
            jax.ShapeDtypeStruct((t, d), F32),
            jax.ShapeDtypeStruct((t, d // 2), I32),
            jax.ShapeDtypeStruct((t, LANES), F32),
            jax.ShapeDtypeStruct((META_ROWS, t), F32),
            jax.ShapeDtypeStruct((1, LANES), F32),
        ],
        compiler_params=_params("arbitrary"),
        name="fox_out",
    )(attn, x2d, mod, w_o, lng.reshape(1, d), lnb.reshape(1, d), wr_hi, wr_lo, br)


def _fox_layer(x2d, u, mod, w_in, b_f, w_o, lng, lnb, router, bsz, seq):
    t, d = x2d.shape
    heads = d // HEAD_DIM
    q_scale = HEAD_DIM ** -0.5 * LOG2E
    w_qkv = jnp.concatenate([w_in[:, :d] * q_scale, w_in[:, d:3 * d]], axis=1).astype(BF16)
    w_f = jnp.pad(w_in[:, 3 * d:], ((0, 0), (0, LANES - heads))).astype(BF16)
    b_f = jnp.pad(b_f, (0, LANES - heads)).reshape(1, LANES)
    qkv, cum = _fox_proj(u, w_qkv, w_f, b_f, seq)
    attn = _fox_attn(qkv, cum, bsz, seq, d)
    return _fox_out(attn, x2d, mod, w_o.astype(BF16), lng, lnb, router, seq)


def _row_copy(src_ref, src_row, dst_ref, dst_row, sem):
    return pltpu.make_async_copy(src_ref.at[pl.ds(src_row, 1)], dst_ref.at[pl.ds(dst_row, 1)], sem)


def _plan_kernel(mt_ref, rs_ref, p_ref):
    mt = mt_ref[...]
    rs = rs_ref[...]
    expert = lax.broadcasted_iota(I32, (N_EXPERTS, mt.shape[1]), 0).astype(F32)
    for k, (e_row, r_row) in enumerate(((META_E1, META_R1), (META_E2, META_R2))):
        start = jnp.sum(jnp.where(expert == mt[e_row:e_row + 1], rs, 0.0), axis=0, keepdims=True)
        p_ref[k:k + 1, :] = (start + mt[r_row:r_row + 1]).astype(I32)


def _plan(meta_t, row_start):
    t = meta_t.shape[1]
    tn = _tile(t, 4096)
    return pl.pallas_call(
        _plan_kernel,
        grid=(t // tn,),
        in_specs=[
            pl.BlockSpec((META_ROWS, tn), lambda i: (0, i)),
            pl.BlockSpec((N_EXPERTS, 1), lambda i: (0, 0)),
        ],
        out_specs=pl.BlockSpec((TOP_K, tn), lambda i: (0, i)),
        out_shape=jax.ShapeDtypeStruct((TOP_K, t), I32),
        compiler_params=_params("arbitrary"),
        name="moe_plan",
    )(meta_t, row_start.astype(F32).reshape(N_EXPERTS, 1))


def _dispatch_kernel(p1_ref, p2_ref, u_ref, xs_in_ref, xs_ref, sem):
    del xs_in_ref
    rows = u_ref.shape[0]
    base = pl.program_id(0) * rows

    def issue(r, _):
        _row_copy(u_ref, r, xs_ref, p1_ref[base + r], sem).start(priority=0)
        _row_copy(u_ref, r, xs_ref, p2_ref[base + r], sem).start(priority=1)
        return 0

    lax.fori_loop(0, rows, issue, 0, unroll=8)

    def drain(r, _):
        _row_copy(u_ref, 0, xs_ref, 0, sem).wait()
        _row_copy(u_ref, 0, xs_ref, 0, sem).wait()
        return 0

    lax.fori_loop(0, rows, drain, 0, unroll=8)


def _dispatch(u_packed, p1, p2, xs_init):
    t, half = u_packed.shape
    ts = _tile(t, 512)
    grid_spec = pltpu.PrefetchScalarGridSpec(
        num_scalar_prefetch=2,
        grid=(t // ts,),
        in_specs=[
            pl.BlockSpec((ts, half), lambda i, p1, p2: (i, 0)),
            pl.BlockSpec(memory_space=pl.ANY),
        ],
        out_specs=pl.BlockSpec(memory_space=pl.ANY),
        scratch_shapes=[pltpu.SemaphoreType.DMA],
    )
    return pl.pallas_call(
        _dispatch_kernel,
        grid_spec=grid_spec,
        out_shape=jax.ShapeDtypeStruct(xs_init.shape, I32),
        input_output_aliases={3: 0},
        compiler_params=_params("arbitrary"),
        name="moe_dispatch",
    )(p1, p2, u_packed, xs_init)


def _experts_kernel(te_ref, ne_ref, nu_ref, xs_ref, wgu_hbm, wd_hbm, o_ref, wgu_st, wd_st, wgu_bf, wd_bf, sems, *,
                    layer):
    i = pl.program_id(0)
    expert = te_ref[i]
    prev = te_ref[jnp.maximum(i - 1, 0)]
    fresh = ((i == 0) | (expert != prev)) & (i < nu_ref[0])

    def fetch(e):
        return (pltpu.make_async_copy(wgu_hbm.at[layer, e], wgu_st, sems.at[0]),
                pltpu.make_async_copy(wd_hbm.at[layer, e], wd_st, sems.at[1]))

    @pl.when(i == 0)
    def _():
        for copy in fetch(expert):
            copy.start()

    @pl.when(fresh)
    def _():
        for copy in fetch(expert):
            copy.wait()
        wgu_bf[...] = wgu_st[...].astype(BF16)
        wd_bf[...] = wd_st[...].astype(BF16)
        following = ne_ref[i]

        @pl.when(following != expert)
        def _():
            for copy in fetch(following):
                copy.start()

    @pl.when(i < nu_ref[0])
    def _():
        hi, lo = _unpack_bf16_pairs(xs_ref[...])
        half = hi.shape[1]
        f = wd_bf.shape[0]
        gu = (jnp.dot(hi, wgu_bf[:half, :], preferred_element_type=F32)
              + jnp.dot(lo, wgu_bf[half:, :], preferred_element_type=F32))
        gate, up = gu[:, :f], gu[:, f:]
        act = gate * (1.0 / (1.0 + jnp.exp(-gate))) * up
        out = jnp.dot(act.astype(BF16), wd_bf[...], preferred_element_type=F32)
        o_ref[...] = _pack_bf16_pairs(out.astype(BF16))

    @pl.when(i >= nu_ref[0])
    def _():
        o_ref[...] = jnp.zeros_like(o_ref)


def _experts(xs, tile_expert, next_expert, n_used, w_gu, w_down, layer, tm):
    n_rows, half = xs.shape
    _, _, d, f2 = w_gu.shape
    f = f2 // 2
    n_tiles = n_rows // tm
    grid_spec = pltpu.PrefetchScalarGridSpec(
        num_scalar_prefetch=3,
        grid=(n_tiles,),
        in_specs=[
            pl.BlockSpec((tm, half), lambda i, te, ne, nu: (jnp.minimum(i, nu[0] - 1), 0)),
            pl.BlockSpec(memory_space=pl.ANY),
            pl.BlockSpec(memory_space=pl.ANY),
        ],
        out_specs=pl.BlockSpec((tm, half), lambda i, te, ne, nu: (i, 0)),
        scratch_shapes=[pltpu.VMEM((d, f2), F32), pltpu.VMEM((f, d), F32),
                        pltpu.VMEM((d, f2), BF16), pltpu.VMEM((f, d), BF16),
                        pltpu.SemaphoreType.DMA((2,))],
    )
    return pl.pallas_call(
        functools.partial(_experts_kernel, layer=layer),
        grid_spec=grid_spec,
        out_shape=jax.ShapeDtypeStruct((n_rows, half), I32),
        compiler_params=_params("arbitrary"),
        name="moe_experts",
    )(tile_expert, next_expert, n_used, xs, w_gu, w_down)


def _combine_kernel(p1_ref, p2_ref, x_ref, meta_ref, mod_ref, lng_ref, lnb_ref, ys_ref, *rest, emit_next):
    if emit_next:
        nmod_ref, o_ref, un_ref, r1, r2, sems = rest
    else:
        o_ref, r1, r2, sems = rest
    i = pl.program_id(0)
    rows = x_ref.shape[0]

    def gather(step, slot):
        base = step * rows

        def issue(r, _):
            _row_copy(ys_ref, p1_ref[base + r], r1.at[slot], r, sems.at[slot]).start(priority=0)
            _row_copy(ys_ref, p2_ref[base + r], r2.at[slot], r, sems.at[slot]).start(priority=1)
            return 0

        lax.fori_loop(0, rows, issue, 0, unroll=8)

    @pl.when(i == 0)
    def _():
        gather(0, 0)

    @pl.when(i + 1 < pl.num_programs(0))
    def _():
        gather(i + 1, (i + 1) % 2)

    slot = i % 2

    def drain(r, _):
        _row_copy(ys_ref, 0, r1.at[slot], 0, sems.at[slot]).wait()
        _row_copy(ys_ref, 0, r2.at[slot], 0, sems.at[slot]).wait()
        return 0

    lax.fori_loop(0, rows, drain, 0, unroll=8)

    meta = meta_ref[...]
    w1 = meta[:, META_W1:META_W1 + 1]
    w2 = meta[:, META_W2:META_W2 + 1]
    hi1, lo1 = _unpack_pairs_f32(r1[slot])
    hi2, lo2 = _unpack_pairs_f32(r2[slot])
    y = jnp.concatenate([w1 * hi1 + w2 * hi2, w1 * lo1 + w2 * lo2], axis=1)
    gate_c = mod_ref[0][5:6]
    x2 = _layer_norm(ALPHA * x_ref[...] + (1.0 + gate_c) * y, lng_ref[...], lnb_ref[...])
    o_ref[...] = x2
    if emit_next:
        nmod = nmod_ref[0]
        un_ref[...] = (x2 * (1.0 + nmod[1:2]) + nmod[0:1]).astype(BF16)


def _combine(x1, meta, mod, next_mod, lng, lnb, ys, p1, p2, seq):
    t, d = x1.shape
    ts = _tile(seq, 256)
    tiles_per_seq = seq // ts
    emit_next = next_mod is not None
    vec = lambda i, p1, p2: (0, 0)
    per_seq = lambda i, p1, p2: (i // tiles_per_seq, 0, 0)
    rows = lambda i, p1, p2: (i, 0)
    in_specs = [
        pl.BlockSpec((ts, d), rows),
        pl.BlockSpec((ts, LANES), rows),
        pl.BlockSpec((1, 6, d), per_seq),
        pl.BlockSpec((1, d), vec),
        pl.BlockSpec((1, d), vec),
        pl.BlockSpec(memory_space=pl.ANY),
    ]
    args = [x1, meta, mod, lng.reshape(1, d), lnb.reshape(1, d), ys]
    out_specs = [pl.BlockSpec((ts, d), rows)]
    out_shape = [jax.ShapeDtypeStruct((t, d), F32)]
    if emit_next:
        in_specs.append(pl.BlockSpec((1, 6, d), per_seq))
        args.append(next_mod)
        out_specs.append(pl.BlockSpec((ts, d), rows))
        out_shape.append(jax.ShapeDtypeStruct((t, d), BF16))
    grid_spec = pltpu.PrefetchScalarGridSpec(
        num_scalar_prefetch=2,
        grid=(t // ts,),
        in_specs=in_specs,
        out_specs=out_specs,
        scratch_shapes=[pltpu.VMEM((2, ts, d // 2), I32), pltpu.VMEM((2, ts, d // 2), I32),
                        pltpu.SemaphoreType.DMA((2,))],
    )
    outs = pl.pallas_call(
        functools.partial(_combine_kernel, emit_next=emit_next),
        grid_spec=grid_spec,
        out_shape=out_shape,
        compiler_params=_params("arbitrary"),
        name="moe_combine",
    )(p1, p2, *args)
    return (outs[0], outs[1]) if emit_next else (outs[0], None)


def _moe_layer(x1, u_packed, meta, meta_t, counts, mod, next_mod, lng, lnb, w_gu, w_down, layer, seq, xs_init):
    t, d = x1.shape
    tm = 512
    n_rows = t * TOP_K + N_EXPERTS * tm
    n_tiles = n_rows // tm

    counts = counts[0, ROUTER_LANE0:ROUTER_LANE0 + N_EXPERTS].astype(I32)
    tiles_per_expert = (counts + tm - 1) // tm
    tile_end = jnp.cumsum(tiles_per_expert)
    row_start = (tile_end - tiles_per_expert) * tm
    n_used = tile_end[-1:]
    tile_ids = jnp.minimum(jnp.arange(n_tiles, dtype=I32), n_used[0] - 1)
    tile_expert = jnp.sum((tile_end[None, :] <= tile_ids[:, None]).astype(I32), axis=1)
    ids = jnp.arange(N_EXPERTS, dtype=I32)
    later = (ids[None, :] > ids[:, None]) & (tiles_per_expert[None, :] > 0)
    following = jnp.min(jnp.where(later, ids[None, :], N_EXPERTS), axis=1)
    following = jnp.where(following == N_EXPERTS, ids, following)
    next_expert = jnp.sum(jnp.where(tile_expert[:, None] == ids[None, :], following[None, :], 0), axis=1)

    p = _plan(meta_t, row_start)
    if xs_init is None:
        xs_init = jnp.zeros((n_rows, d // 2), I32)
    xs = _dispatch(u_packed, p[0], p[1], xs_init)
    ys = _experts(xs, tile_expert, next_expert, n_used, w_gu, w_down, layer, tm)
    x2, u_next = _combine(x1, meta, mod, next_mod, lng, lnb, ys, p[0], p[1], seq)
    return x2, u_next, xs


def kernel(x, c, ada_w, ada_b, ln_g, ln_b, pool_w, pool_scale, fox_w_in, fox_b_f, fox_w_o,
           router_w_group, router_b_group, router_w_expert, router_b_expert, moe_w_gate_up, moe_w_down):
    bsz, seq, d = x.shape
    mods = _adaln(c, ada_w, ada_b)
    n_mixers = 2
    is_fox = lambda l: l % n_mixers == 1
    u_mixer, xs_buf = None, None
    for l in range(DEPTH):
        mod = mods[l]
        router = _router_weights(router_w_group[l], router_b_group[l], router_w_expert[l], router_b_expert[l])
        j = l // n_mixers
        if is_fox(l):
            assert u_mixer is not None, "an attention layer takes its modulated input from the layer before it"
            x1, u_packed, meta, meta_t, counts = _fox_layer(x.reshape(bsz * seq, d), u_mixer, mod, fox_w_in[j],
                                                            fox_b_f[j], fox_w_o[j], ln_g[l, 0], ln_b[l, 0], router,
                                                            bsz, seq)
        else:
            x1, u_packed, meta, meta_t, counts = _pool_layer(x, mod, pool_w[j], pool_scale[j], ln_g[l, 0],
                                                             ln_b[l, 0], router)
            x1 = x1.reshape(bsz * seq, d)
        next_mod = mods[l + 1] if l + 1 < DEPTH and is_fox(l + 1) else None
        x, u_mixer, xs_buf = _moe_layer(x1, u_packed, meta, meta_t, counts, mod, next_mod, ln_g[l, 1], ln_b[l, 1],
                                        moe_w_gate_up, moe_w_down, l, seq, xs_buf)
        x = x.reshape(bsz, seq, d)
    return x
```

```python
import functools

import jax
import jax.numpy as jnp
from jax import lax
from jax.experimental import pallas as pl
from jax.experimental.pallas import tpu as pltpu

F32 = jnp.float32
BF16 = jnp.bfloat16
I32 = jnp.int32
HIGHEST = lax.Precision.HIGHEST

DEPTH = 2
POOL_WINDOWS = (2, 4, 8, 16)
POOL_HALO = 16
HEAD_DIM = 128
N_GROUPS = 8
EXPERTS_PER_GROUP = 8
N_EXPERTS = N_GROUPS * EXPERTS_PER_GROUP
TOP_K = 2
ALPHA = (2 * DEPTH) ** 0.25
LN_EPS = 1e-5
LOG2E = 1.4426950408889634

LANES = 128
V7X_VMEM_BYTES = 64 * 1024 * 1024
VMEM_LIMIT_BYTES = V7X_VMEM_BYTES - 8 * 1024 * 1024

META_E1, META_E2, META_W1, META_W2, META_R1, META_R2 = range(6)
META_ROWS = 8
ROUTER_LANE0 = N_GROUPS


def _params(*semantics):
    return pltpu.CompilerParams(dimension_semantics=semantics, vmem_limit_bytes=VMEM_LIMIT_BYTES)


def _tile(n, pref):
    t = min(n, pref)
    while n % t:
        t //= 2
    return t


def _adaln_kernel(c_ref, w_ref, b_ref, o_ref):
    c = c_ref[...]
    c_act = c * (1.0 / (1.0 + jnp.exp(-c)))
    o_ref[0] = jnp.dot(c_act, w_ref[0], precision=HIGHEST, preferred_element_type=F32) + b_ref[0]


def _adaln(c, ada_w, ada_b):
    n_layers, d, n = ada_w.shape
    b = c.shape[0]
    rows = -(-b // 8) * 8
    c_pad = jnp.pad(c, ((0, rows - b), (0, 0)))
    tn = _tile(n, 1024)
    out = pl.pallas_call(
        _adaln_kernel,
        grid=(n_layers, n // tn),
        in_specs=[
            pl.BlockSpec((rows, d), lambda l, j: (0, 0)),
            pl.BlockSpec((1, d, tn), lambda l, j: (l, 0, j)),
            pl.BlockSpec((1, 1, tn), lambda l, j: (l, 0, j)),
        ],
        out_specs=pl.BlockSpec((1, rows, tn), lambda l, j: (l, 0, j)),
        out_shape=jax.ShapeDtypeStruct((n_layers, rows, n), F32),
        compiler_params=_params("arbitrary", "arbitrary"),
        name="adaln_mod",
    )(c_pad, ada_w, ada_b.reshape(n_layers, 1, n))
    return out[:, :b].reshape(n_layers, b, 6, d)


def _layer_norm(h, g, b):
    mu = jnp.mean(h, axis=-1, keepdims=True)
    d = h - mu
    var = jnp.mean(d * d, axis=-1, keepdims=True)
    return d * lax.rsqrt(var + LN_EPS) * g + b


def _pack_bf16_pairs(u_hi):
    half = u_hi.shape[1] // 2
    bits = lax.bitcast_convert_type(u_hi.astype(F32), I32)
    return bits[:, :half] | lax.shift_right_logical(bits[:, half:], 16)


def _unpack_pairs_f32(words):
    hi = lax.bitcast_convert_type(words & jnp.int32(-65536), F32)
    lo = lax.bitcast_convert_type(lax.shift_left(words, 16), F32)
    return hi, lo


def _unpack_bf16_pairs(words):
    hi, lo = _unpack_pairs_f32(words)
    return hi.astype(BF16), lo.astype(BF16)


def _route(logits, cnt_ref):
    rows = logits.shape[0]
    lane = lax.broadcasted_iota(I32, (rows, LANES), 1).astype(F32)
    neg_inf = jnp.float32(-jnp.inf)
    big = jnp.float32(2 * LANES)

    g_mask = lane < N_GROUPS
    gl = jnp.where(g_mask, logits, neg_inf)
    g_max = jnp.max(gl, axis=-1, keepdims=True)
    g_idx = jnp.min(jnp.where(gl == g_max, lane, big), axis=-1, keepdims=True)
    p_group = 1.0 / jnp.sum(jnp.where(g_mask, jnp.exp(gl - g_max), 0.0), axis=-1, keepdims=True)

    lo = ROUTER_LANE0 + EXPERTS_PER_GROUP * g_idx
    e_mask = (lane >= lo) & (lane < lo + EXPERTS_PER_GROUP)
    el = jnp.where(e_mask, logits, neg_inf)
    v1 = jnp.max(el, axis=-1, keepdims=True)
    i1 = jnp.min(jnp.where(el == v1, lane, big), axis=-1, keepdims=True)
    el2 = jnp.where(lane == i1, neg_inf, el)
    v2 = jnp.max(el2, axis=-1, keepdims=True)
    i2 = jnp.min(jnp.where(el2 == v2, lane, big), axis=-1, keepdims=True)
    t = jnp.exp(v2 - v1)
    w1 = p_group / (1.0 + t)
    w2 = p_group * t / (1.0 + t)

    sel1 = lane == i1
    sel2 = lane == i2
    onehot = jnp.where(sel1 | sel2, 1.0, 0.0)
    r_idx = lax.broadcasted_iota(I32, (rows, rows), 0)
    c_idx = lax.broadcasted_iota(I32, (rows, rows), 1)
    strict_lower = jnp.where(c_idx < r_idx, 1.0, 0.0).astype(BF16)
    before = jnp.dot(strict_lower, onehot.astype(BF16), preferred_element_type=F32) + cnt_ref[...]
    r1 = jnp.sum(jnp.where(sel1, before, 0.0), axis=-1, keepdims=True)
    r2 = jnp.sum(jnp.where(sel2, before, 0.0), axis=-1, keepdims=True)
    cnt_ref[...] += jnp.sum(onehot, axis=0, keepdims=True)

    cols = (i1 - ROUTER_LANE0, i2 - ROUTER_LANE0, w1, w2, r1, r2)
    meta = jnp.zeros((rows, LANES), F32)
    for k, col in enumerate(cols):
        meta = jnp.where(lane == k, col, meta)
    return meta


def _post_mixer(x, y, mod, lng, lnb, wr_hi, wr_lo, br, x1_ref, up_ref, meta_ref, metat_ref, cnt_ref):
    gate_t, shift_c, scale_c = mod[2:3], mod[3:4], mod[4:5]
    x1 = _layer_norm(ALPHA * x + (1.0 + gate_t) * y, lng, lnb)
    x1_ref[...] = x1
    u = x1 * (1.0 + scale_c) + shift_c
    u_hi = u.astype(BF16)
    u_lo = (u - u_hi.astype(F32)).astype(BF16)
    up_ref[...] = _pack_bf16_pairs(u_hi)
    logits = (jnp.dot(u_hi, wr_hi, preferred_element_type=F32)
              + jnp.dot(u_lo, wr_hi, preferred_element_type=F32)
              + jnp.dot(u_hi, wr_lo, preferred_element_type=F32)) + br
    meta = _route(logits, cnt_ref)
    meta_ref[...] = meta
    metat_ref[...] = meta.T[:META_ROWS, :]


def _router_weights(w_rg, b_rg, w_re, b_re):
    d = w_rg.shape[0]
    pad = LANES - N_GROUPS - N_EXPERTS
    w = jnp.concatenate([w_rg, w_re, jnp.zeros((d, pad), F32)], axis=1)
    b = jnp.concatenate([b_rg, b_re, jnp.zeros((pad,), F32)]).reshape(1, LANES)
    w_hi = w.astype(BF16)
    w_lo = (w - w_hi.astype(F32)).astype(BF16)
    return w_hi, w_lo, b


def _pool_kernel(x_ref, halo_ref, mod_ref, wp_ref, ps_ref, lng_ref, lnb_ref, wrh_ref, wrl_ref, br_ref,
                 x1_ref, up_ref, meta_ref, metat_ref, cnt_ref):
    b = pl.program_id(0)
    i = pl.program_id(1)

    @pl.when((b == 0) & (i == 0))
    def _():
        cnt_ref[...] = jnp.zeros_like(cnt_ref)

    mod = mod_ref[0]
    shift_t, scale_t = mod[0:1], mod[1:2]
    x = x_ref[0]
    rows, d = x.shape
    group = d // len(POOL_WINDOWS)
    u = x * (1.0 + scale_t) + shift_t
    halo = halo_ref[0] * (1.0 + scale_t) + shift_t
    halo = jnp.where(i > 0, halo, 0.0)
    pos = i * rows + lax.broadcasted_iota(I32, (rows, 1), 0)

    ys = []
    for g, w in enumerate(POOL_WINDOWS):
        cols = slice(g * group, (g + 1) * group)
        s = jnp.concatenate([halo[:, cols], u[:, cols]], axis=0)
        k = 1
        while k < w:
            s = s + pltpu.roll(s, k, axis=0)
            k *= 2
        inv_count = 1.0 / jnp.minimum(pos + 1, w).astype(F32)
        pooled = s[POOL_HALO:, :] * inv_count - u[:, cols]
        mixed = jnp.dot(pooled.astype(BF16), wp_ref[g], preferred_element_type=F32)
        ys.append(mixed * ps_ref[:, cols])
    y = jnp.concatenate(ys, axis=-1)

    _post_mixer(x, y, mod, lng_ref[...], lnb_ref[...], wrh_ref[...], wrl_ref[...], br_ref[...],
                x1_ref.at[0], up_ref, meta_ref, metat_ref, cnt_ref)


def _pool_layer(x, mod, pool_w, pool_scale, lng, lnb, router):
    bsz, s, d = x.shape
    ts = _tile(s, 256)
    nsb = s // ts
    hb = ts // POOL_HALO
    wr_hi, wr_lo, br = router
    vec = lambda b, i: (0, 0)
    return pl.pallas_call(
        _pool_kernel,
        grid=(bsz, nsb),
        in_specs=[
            pl.BlockSpec((1, ts, d), lambda b, i: (b, i, 0)),
            pl.BlockSpec((1, POOL_HALO, d), lambda b, i: (b, jnp.maximum(i * hb - 1, 0), 0)),
            pl.BlockSpec((1, 6, d), lambda b, i: (b, 0, 0)),
            pl.BlockSpec(pool_w.shape, lambda b, i: (0, 0, 0)),
            pl.BlockSpec((1, d), vec),
            pl.BlockSpec((1, d), vec),
            pl.BlockSpec((1, d), vec),
            pl.BlockSpec((d, LANES), vec),
            pl.BlockSpec((d, LANES), vec),
            pl.BlockSpec((1, LANES), vec),
        ],
        out_specs=[
            pl.BlockSpec((1, ts, d), lambda b, i: (b, i, 0)),
            pl.BlockSpec((ts, d // 2), lambda b, i: (b * nsb + i, 0)),
            pl.BlockSpec((ts, LANES), lambda b, i: (b * nsb + i, 0)),
            pl.BlockSpec((META_ROWS, ts), lambda b, i: (0, b * nsb + i)),
            pl.BlockSpec((1, LANES), vec),
        ],
        out_shape=[
            jax.ShapeDtypeStruct((bsz, s, d), F32),
            jax.ShapeDtypeStruct((bsz * s, d // 2), I32),
            jax.ShapeDtypeStruct((bsz * s, LANES), F32),
            jax.ShapeDtypeStruct((META_ROWS, bsz * s), F32),
            jax.ShapeDtypeStruct((1, LANES), F32),
        ],
        compiler_params=_params("arbitrary", "arbitrary"),
        name="pool_layer",
    )(x, x, mod, pool_w.astype(BF16), pool_scale.reshape(1, d), lng.reshape(1, d), lnb.reshape(1, d),
      wr_hi, wr_lo, br)


def _fox_proj_kernel(u_ref, w_ref, wf_ref, bf_ref, qkv_ref, cum_ref, carry_scr, *, tiles_per_seq):
    j = pl.program_id(0)
    i = pl.program_id(1)
    u = u_ref[...]

    @pl.when((j == 0) & (i % tiles_per_seq == 0))
    def _():
        carry_scr[...] = jnp.zeros_like(carry_scr)

    @pl.when(j == 0)
    def _():
        f_logit = jnp.dot(u, wf_ref[...], preferred_element_type=F32) + bf_ref[...]
        log_f = jnp.minimum(f_logit, 0.0) - jnp.log1p(jnp.exp(-jnp.abs(f_logit)))
        rows = log_f.shape[0]
        r_idx = lax.broadcasted_iota(I32, (rows, rows), 0)
        c_idx = lax.broadcasted_iota(I32, (rows, rows), 1)
        lower = jnp.where(c_idx <= r_idx, 1.0, 0.0).astype(BF16)
        cum = carry_scr[...]
        for piece in _split3(log_f):
            cum = cum + jnp.dot(lower, piece.astype(BF16), preferred_element_type=F32)
        cum_ref[...] = cum
        carry_scr[...] = cum[rows - 1:rows, :]

    res = jnp.dot(u, w_ref[...], preferred_element_type=F32).astype(BF16)
    for hb in range(qkv_ref.shape[0]):
        qkv_ref[hb] = res[:, hb * HEAD_DIM:(hb + 1) * HEAD_DIM]


def _fox_proj(u, w_qkv, w_f, b_f, seq):
    t, d = u.shape
    n = w_qkv.shape[1]
    tm = _tile(seq, 512)
    tn = _tile(n, 1536)
    tiles_per_seq = seq // tm
    n_i = t // tm
    return pl.pallas_call(
        functools.partial(_fox_proj_kernel, tiles_per_seq=tiles_per_seq),
        grid=(n // tn, n_i),
        in_specs=[
            pl.BlockSpec((tm, d), lambda j, i: (i, 0)),
            pl.BlockSpec((d, tn), lambda j, i: (0, j)),
            pl.BlockSpec((d, LANES), lambda j, i: (0, 0)),
            pl.BlockSpec((1, LANES), lambda j, i: (0, 0)),
        ],
        out_specs=[
            pl.BlockSpec((tn // HEAD_DIM, tm, HEAD_DIM), lambda j, i: (j, i, 0)),
            pl.BlockSpec((tm, LANES), lambda j, i: (jnp.where(j == 0, i, n_i - 1), 0)),
        ],
        out_shape=[
            jax.ShapeDtypeStruct((n // HEAD_DIM, t, HEAD_DIM), BF16),
            jax.ShapeDtypeStruct((t, LANES), F32),
        ],
        scratch_shapes=[pltpu.VMEM((1, LANES), F32)],
        compiler_params=_params("arbitrary", "arbitrary"),
        name="fox_proj",
    )(u, w_qkv, w_f, b_f)


def _split3(col):
    hi = col.astype(BF16).astype(F32)
    rest = col - hi
    mid = rest.astype(BF16).astype(F32)
    lo = (rest - mid).astype(BF16).astype(F32)
    return hi, mid, lo


def _head_column(block, h):
    lane = lax.broadcasted_iota(I32, block.shape, 1)
    return jnp.sum(jnp.where(lane == h, block, 0.0), axis=-1, keepdims=True) * LOG2E


def _fox_attn_kernel(q_ref, k_ref, v_ref, cq_ref, ck_ref, o_ref,
                     qa_scr, kx_scr, vx_scr, s_a, s_b, rm_a, rm_b, m_scr, acc_scr, *, build_rows):
    h = pl.program_id(1)
    qi = pl.program_id(2)
    q_ref, k_ref, v_ref = q_ref.at[0], k_ref.at[0], v_ref.at[0]
    tq = q_ref.shape[0]
    tk = vx_scr.shape[0]
    seq = k_ref.shape[0]
    lane = lax.broadcasted_iota(I32, (build_rows, LANES), 1)

    @pl.when(qi == 0)
    def _():
        def build(i, _):
            start = pl.multiple_of(i * build_rows, build_rows)
            hi, mid, lo = _split3(_head_column(ck_ref[pl.ds(start, build_rows), :], h))
            kx = jnp.where(lane < 3, 1.0,
                           jnp.where(lane == 3, -hi, jnp.where(lane == 4, -mid, jnp.where(lane == 5, -lo, 0.0))))
            kx_scr[pl.ds(start, build_rows), :] = kx.astype(BF16)
            return 0

        lax.fori_loop(0, seq // build_rows, build, 0)
        lane_v = lax.broadcasted_iota(I32, (tk, LANES), 1)
        vx_scr[...] = jnp.where(lane_v == 0, 1.0, 0.0).astype(BF16)

    lane_q = lax.broadcasted_iota(I32, (tq, LANES), 1)
    hi, mid, lo = _split3(_head_column(cq_ref[...], h))
    qx = jnp.where(lane_q == 0, hi, jnp.where(lane_q == 1, mid, jnp.where(lane_q == 2, lo,
                                                                            jnp.where(lane_q < 6, 1.0, 0.0))))
    qa_scr[:, :HEAD_DIM] = q_ref[...]
    qa_scr[:, HEAD_DIM:] = qx.astype(BF16)
    m_scr[...] = jnp.full(m_scr.shape, -jnp.inf, F32)
    acc_scr[...] = jnp.zeros(acc_scr.shape, F32)

    def produce(c, s_dst, rm_dst, masked):
        start = pl.multiple_of(c * tk, tk)
        kk = jnp.concatenate([k_ref[pl.ds(start, tk), :], kx_scr[pl.ds(start, tk), :]], axis=1)
        s = lax.dot_general(qa_scr[...], kk, (((1,), (1,)), ((), ())), preferred_element_type=F32)
        if masked:
            q_pos = qi * tq + lax.broadcasted_iota(I32, (tq, tk), 0)
            k_pos = start + lax.broadcasted_iota(I32, (tq, tk), 1)
            s = jnp.where(k_pos <= q_pos, s, -jnp.inf)
        s_dst[...] = s
        rm_dst[...] = jnp.broadcast_to(jnp.max(s, axis=-1, keepdims=True), rm_dst.shape)

    def consume(c, s_src, rm_src):
        start = pl.multiple_of(c * tk, tk)
        vv = jnp.concatenate([v_ref[pl.ds(start, tk), :], vx_scr[...]], axis=1)
        m_old = m_scr[...]
        m_new = jnp.maximum(m_old, rm_src[...])
        alpha = jnp.exp2(m_old - m_new)
        m_scr[...] = m_new
        p = jnp.concatenate(
            [jnp.exp2(s_src[:, j * LANES:(j + 1) * LANES] - m_new).astype(BF16) for j in range(tk // LANES)],
            axis=1)
        pv = jnp.dot(p, vv, preferred_element_type=F32)
        for j in range(2 * HEAD_DIM // LANES):
            blk = slice(j * LANES, (j + 1) * LANES)
            acc_scr[:, blk] = alpha * acc_scr[:, blk] + pv[:, blk]

    produce(0, s_a, rm_a, True)
    produce(1, s_b, rm_b, True)

    def pair(j, masked):
        c = 2 * j
        consume(c, s_a, rm_a)
        produce(c + 2, s_a, rm_a, masked)
        consume(c + 1, s_b, rm_b)
        produce(c + 3, s_b, rm_b, masked)

    def two_unmasked_pairs(jj, _):
        pair(2 * jj, False)
        pair(2 * jj + 1, False)
        return 0

    n_unmasked = jnp.maximum(qi - 1, 0)
    lax.fori_loop(0, n_unmasked // 2, two_unmasked_pairs, 0)

    @pl.when(n_unmasked % 2 == 1)
    def _():
        pair(n_unmasked - 1, False)

    @pl.when(qi > 0)
    def _():
        pair(qi - 1, True)

    consume(2 * qi, s_a, rm_a)
    consume(2 * qi + 1, s_b, rm_b)

    acc = acc_scr[...]
    o_ref[...] = (acc[:, :HEAD_DIM] / acc[:, HEAD_DIM:HEAD_DIM + 1]).astype(BF16)


def _fox_attn(qkv, cum, bsz, seq, d):
    heads = d // HEAD_DIM
    tq = _tile(seq, 1024)
    tk = tq // 2
    nq = seq // tq
    return pl.pallas_call(
        functools.partial(_fox_attn_kernel, build_rows=tk),
        grid=(bsz, heads, nq),
        in_specs=[
            pl.BlockSpec((1, tq, HEAD_DIM), lambda b, h, i: (h, b * nq + i, 0)),
            pl.BlockSpec((1, seq, HEAD_DIM), lambda b, h, i: (heads + h, b, 0)),
            pl.BlockSpec((1, seq, HEAD_DIM), lambda b, h, i: (2 * heads + h, b, 0)),
            pl.BlockSpec((tq, LANES), lambda b, h, i: (b * nq + i, 0)),
            pl.BlockSpec((seq, LANES), lambda b, h, i: (b, 0)),
        ],
        out_specs=pl.BlockSpec((tq, HEAD_DIM), lambda b, h, i: (b * nq + i, h)),
        out_shape=jax.ShapeDtypeStruct((bsz * seq, d), BF16),
        scratch_shapes=[
            pltpu.VMEM((tq, 2 * HEAD_DIM), BF16),
            pltpu.VMEM((seq, LANES), BF16),
            pltpu.VMEM((tk, LANES), BF16),
            pltpu.VMEM((tq, tk), F32),
            pltpu.VMEM((tq, tk), F32),
            pltpu.VMEM((tq, LANES), F32),
            pltpu.VMEM((tq, LANES), F32),
            pltpu.VMEM((tq, LANES), F32),
            pltpu.VMEM((tq, 2 * HEAD_DIM), F32),
        ],
        compiler_params=_params("arbitrary", "arbitrary", "arbitrary"),
        name="fox_attn",
    )(qkv, qkv, qkv, cum, cum)


def _fox_out_kernel(a_ref, x_ref, mod_ref, wo_ref, lng_ref, lnb_ref, wrh_ref, wrl_ref, br_ref,
                    x1_ref, up_ref, meta_ref, metat_ref, cnt_ref):
    @pl.when(pl.program_id(0) == 0)
    def _():
        cnt_ref[...] = jnp.zeros_like(cnt_ref)

    y = jnp.dot(a_ref[...], wo_ref[...], preferred_element_type=F32)
    _post_mixer(x_ref[...], y, mod_ref[0], lng_ref[...], lnb_ref[...], wrh_ref[...], wrl_ref[...],
                br_ref[...], x1_ref, up_ref, meta_ref, metat_ref, cnt_ref)


def _fox_out(attn, x2d, mod, w_o, lng, lnb, router, seq):
    t, d = x2d.shape
    tm = _tile(seq, 256)
    tiles_per_seq = seq // tm
    wr_hi, wr_lo, br = router
    vec = lambda i: (0, 0)
    return pl.pallas_call(
        _fox_out_kernel,
        grid=(t // tm,),
        in_specs=[
            pl.BlockSpec((tm, d), lambda i: (i, 0)),
            pl.BlockSpec((tm, d), lambda i: (i, 0)),
            pl.BlockSpec((1, 6, d), lambda i: (i // tiles_per_seq, 0, 0)),
            pl.BlockSpec((d, d), vec),
            pl.BlockSpec((1, d), vec),
            pl.BlockSpec((1, d), vec),
            pl.BlockSpec((d, LANES), vec),
            pl.BlockSpec((d, LANES), vec),
            pl.BlockSpec((1, LANES), vec),
        ],
        out_specs=[
            pl.BlockSpec((tm, d), lambda i: (i, 0)),
            pl.BlockSpec((tm, d // 2), lambda i: (i, 0)),
            pl.BlockSpec((tm, LANES), lambda i: (i, 0)),
            pl.BlockSpec((META_ROWS, tm), lambda i: (0, i)),
            pl.BlockSpec((1, LANES), vec),
        ],
        out_shape=[
            jax.ShapeDtypeStruct((t, d), F32),
            jax.ShapeDtypeStruct((t, d // 2), I32),
            jax.ShapeDtypeStruct((t, LANES), F32),
            jax.ShapeDtypeStruct((META_ROWS, t), F32),
            jax.ShapeDtypeStruct((1, LANES), F32),
        ],
        compiler_params=_params("arbitrary"),
        name="fox_out",
    )(attn, x2d, mod, w_o, lng.reshape(1, d), lnb.reshape(1, d), wr_hi, wr_lo, br)


def _fox_layer(x2d, u, mod, w_in, b_f, w_o, lng, lnb, router, bsz, seq):
    t, d = x2d.shape
    heads = d // HEAD_DIM
    q_scale = HEAD_DIM ** -0.5 * LOG2E
    w_qkv = jnp.concatenate([w_in[:, :d] * q_scale, w_in[:, d:3 * d]], axis=1).astype(BF16)
    w_f = jnp.pad(w_in[:, 3 * d:], ((0, 0), (0, LANES - heads))).astype(BF16)
    b_f = jnp.pad(b_f, (0, LANES - heads)).reshape(1, LANES)
    qkv, cum = _fox_proj(u, w_qkv, w_f, b_f, seq)
    attn = _fox_attn(qkv, cum, bsz, seq, d)
    return _fox_out(attn, x2d, mod, w_o.astype(BF16), lng, lnb, router, seq)


def _row_copy(src_ref, src_row, dst_ref, dst_row, sem):
    return pltpu.make_async_copy(src_ref.at[pl.ds(src_row, 1)], dst_ref.at[pl.ds(dst_row, 1)], sem)


def _plan_kernel(mt_ref, rs_ref, p_ref):
    mt = mt_ref[...]
    rs = rs_ref[...]
    expert = lax.broadcasted_iota(I32, (N_EXPERTS, mt.shape[1]), 0).astype(F32)
    for k, (e_row, r_row) in enumerate(((META_E1, META_R1), (META_E2, META_R2))):
        start = jnp.sum(jnp.where(expert == mt[e_row:e_row + 1], rs, 0.0), axis=0, keepdims=True)
        p_ref[k:k + 1, :] = (start + mt[r_row:r_row + 1]).astype(I32)


def _plan(meta_t, row_start):
    t = meta_t.shape[1]
    tn = _tile(t, 4096)
    return pl.pallas_call(
        _plan_kernel,
        grid=(t // tn,),
        in_specs=[
            pl.BlockSpec((META_ROWS, tn), lambda i: (0, i)),
            pl.BlockSpec((N_EXPERTS, 1), lambda i: (0, 0)),
        ],
        out_specs=pl.BlockSpec((TOP_K, tn), lambda i: (0, i)),
        out_shape=jax.ShapeDtypeStruct((TOP_K, t), I32),
        compiler_params=_params("arbitrary"),
        name="moe_plan",
    )(meta_t, row_start.astype(F32).reshape(N_EXPERTS, 1))


def _dispatch_kernel(p1_ref, p2_ref, u_ref, xs_in_ref, xs_ref, sem):
    del xs_in_ref
    rows = u_ref.shape[0]
    base = pl.program_id(0) * rows

    def issue(r, _):
        _row_copy(u_ref, r, xs_ref, p1_ref[base + r], sem).start(priority=0)
        _row_copy(u_ref, r, xs_ref, p2_ref[base + r], sem).start(priority=1)
        return 0

    lax.fori_loop(0, rows, issue, 0, unroll=8)

    def drain(r, _):
        _row_copy(u_ref, 0, xs_ref, 0, sem).wait()
        _row_copy(u_ref, 0, xs_ref, 0, sem).wait()
        return 0

    lax.fori_loop(0, rows, drain, 0, unroll=8)


def _dispatch(u_packed, p1, p2, xs_init):
    t, half = u_packed.shape
    ts = _tile(t, 512)
    grid_spec = pltpu.PrefetchScalarGridSpec(
        num_scalar_prefetch=2,
        grid=(t // ts,),
        in_specs=[
            pl.BlockSpec((ts, half), lambda i, p1, p2: (i, 0)),
            pl.BlockSpec(memory_space=pl.ANY),
        ],
        out_specs=pl.BlockSpec(memory_space=pl.ANY),
        scratch_shapes=[pltpu.SemaphoreType.DMA],
    )
    return pl.pallas_call(
        _dispatch_kernel,
        grid_spec=grid_spec,
        out_shape=jax.ShapeDtypeStruct(xs_init.shape, I32),
        input_output_aliases={3: 0},
        compiler_params=_params("arbitrary"),
        name="moe_dispatch",
    )(p1, p2, u_packed, xs_init)


def _experts_kernel(te_ref, ne_ref, nu_ref, xs_ref, wgu_hbm, wd_hbm, o_ref, wgu_st, wd_st, wgu_bf, wd_bf, sems, *,
                    layer):
    i = pl.program_id(0)
    expert = te_ref[i]
    prev = te_ref[jnp.maximum(i - 1, 0)]
    fresh = ((i == 0) | (expert != prev)) & (i < nu_ref[0])

    def fetch(e):
        return (pltpu.make_async_copy(wgu_hbm.at[layer, e], wgu_st, sems.at[0]),
                pltpu.make_async_copy(wd_hbm.at[layer, e], wd_st, sems.at[1]))

    @pl.when(i == 0)
    def _():
        for copy in fetch(expert):
            copy.start()

    @pl.when(fresh)
    def _():
        for copy in fetch(expert):
            copy.wait()
        wgu_bf[...] = wgu_st[...].astype(BF16)
        wd_bf[...] = wd_st[...].astype(BF16)
        following = ne_ref[i]

        @pl.when(following != expert)
        def _():
            for copy in fetch(following):
                copy.start()

    @pl.when(i < nu_ref[0])
    def _():
        hi, lo = _unpack_bf16_pairs(xs_ref[...])
        half = hi.shape[1]
        f = wd_bf.shape[0]
        gu = (jnp.dot(hi, wgu_bf[:half, :], preferred_element_type=F32)
              + jnp.dot(lo, wgu_bf[half:, :], preferred_element_type=F32))
        gate, up = gu[:, :f], gu[:, f:]
        act = gate * (1.0 / (1.0 + jnp.exp(-gate))) * up
        out = jnp.dot(act.astype(BF16), wd_bf[...], preferred_element_type=F32)
        o_ref[...] = _pack_bf16_pairs(out.astype(BF16))

    @pl.when(i >= nu_ref[0])
    def _():
        o_ref[...] = jnp.zeros_like(o_ref)


def _experts(xs, tile_expert, next_expert, n_used, w_gu, w_down, layer, tm):
    n_rows, half = xs.shape
    _, _, d, f2 = w_gu.shape
    f = f2 // 2
    n_tiles = n_rows // tm
    grid_spec = pltpu.PrefetchScalarGridSpec(
        num_scalar_prefetch=3,
        grid=(n_tiles,),
        in_specs=[
            pl.BlockSpec((tm, half), lambda i, te, ne, nu: (jnp.minimum(i, nu[0] - 1), 0)),
            pl.BlockSpec(memory_space=pl.ANY),
            pl.BlockSpec(memory_space=pl.ANY),
        ],
        out_specs=pl.BlockSpec((tm, half), lambda i, te, ne, nu: (i, 0)),
        scratch_shapes=[pltpu.VMEM((d, f2), F32), pltpu.VMEM((f, d), F32),
                        pltpu.VMEM((d, f2), BF16), pltpu.VMEM((f, d), BF16),
                        pltpu.SemaphoreType.DMA((2,))],
    )
    return pl.pallas_call(
        functools.partial(_experts_kernel, layer=layer),
        grid_spec=grid_spec,
        out_shape=jax.ShapeDtypeStruct((n_rows, half), I32),
        compiler_params=_params("arbitrary"),
        name="moe_experts",
    )(tile_expert, next_expert, n_used, xs, w_gu, w_down)


def _combine_kernel(p1_ref, p2_ref, x_ref, meta_ref, mod_ref, lng_ref, lnb_ref, ys_ref, *rest, emit_next):
    if emit_next:
        nmod_ref, o_ref, un_ref, r1, r2, sems = rest
    else:
        o_ref, r1, r2, sems = rest
    i = pl.program_id(0)
    rows = x_ref.shape[0]

    def gather(step, slot):
        base = step * rows

        def issue(r, _):
            _row_copy(ys_ref, p1_ref[base + r], r1.at[slot], r, sems.at[slot]).start(priority=0)
            _row_copy(ys_ref, p2_ref[base + r], r2.at[slot], r, sems.at[slot]).start(priority=1)
            return 0

        lax.fori_loop(0, rows, issue, 0, unroll=8)

    @pl.when(i == 0)
    def _():
        gather(0, 0)

    @pl.when(i + 1 < pl.num_programs(0))
    def _():
        gather(i + 1, (i + 1) % 2)

    slot = i % 2

    def drain(r, _):
        _row_copy(ys_ref, 0, r1.at[slot], 0, sems.at[slot]).wait()
        _row_copy(ys_ref, 0, r2.at[slot], 0, sems.at[slot]).wait()
        return 0

    lax.fori_loop(0, rows, drain, 0, unroll=8)

    meta = meta_ref[...]
    w1 = meta[:, META_W1:META_W1 + 1]
    w2 = meta[:, META_W2:META_W2 + 1]
    hi1, lo1 = _unpack_pairs_f32(r1[slot])
    hi2, lo2 = _unpack_pairs_f32(r2[slot])
    y = jnp.concatenate([w1 * hi1 + w2 * hi2, w1 * lo1 + w2 * lo2], axis=1)
    gate_c = mod_ref[0][5:6]
    x2 = _layer_norm(ALPHA * x_ref[...] + (1.0 + gate_c) * y, lng_ref[...], lnb_ref[...])
    o_ref[...] = x2
    if emit_next:
        nmod = nmod_ref[0]
        un_ref[...] = (x2 * (1.0 + nmod[1:2]) + nmod[0:1]).astype(BF16)


def _combine(x1, meta, mod, next_mod, lng, lnb, ys, p1, p2, seq):
    t, d = x1.shape
    ts = _tile(seq, 256)
    tiles_per_seq = seq // ts
    emit_next = next_mod is not None
    vec = lambda i, p1, p2: (0, 0)
    per_seq = lambda i, p1, p2: (i // tiles_per_seq, 0, 0)
    rows = lambda i, p1, p2: (i, 0)
    in_specs = [
        pl.BlockSpec((ts, d), rows),
        pl.BlockSpec((ts, LANES), rows),
        pl.BlockSpec((1, 6, d), per_seq),
        pl.BlockSpec((1, d), vec),
        pl.BlockSpec((1, d), vec),
        pl.BlockSpec(memory_space=pl.ANY),
    ]
    args = [x1, meta, mod, lng.reshape(1, d), lnb.reshape(1, d), ys]
    out_specs = [pl.BlockSpec((ts, d), rows)]
    out_shape = [jax.ShapeDtypeStruct((t, d), F32)]
    if emit_next:
        in_specs.append(pl.BlockSpec((1, 6, d), per_seq))
        args.append(next_mod)
        out_specs.append(pl.BlockSpec((ts, d), rows))
        out_shape.append(jax.ShapeDtypeStruct((t, d), BF16))
    grid_spec = pltpu.PrefetchScalarGridSpec(
        num_scalar_prefetch=2,
        grid=(t // ts,),
        in_specs=in_specs,
        out_specs=out_specs,
        scratch_shapes=[pltpu.VMEM((2, ts, d // 2), I32), pltpu.VMEM((2, ts, d // 2), I32),
                        pltpu.SemaphoreType.DMA((2,))],
    )
    outs = pl.pallas_call(
        functools.partial(_combine_kernel, emit_next=emit_next),
        grid_spec=grid_spec,
        out_shape=out_shape,
        compiler_params=_params("arbitrary"),
        name="moe_combine",
    )(p1, p2, *args)
    return (outs[0], outs[1]) if emit_next else (outs[0], None)


def _moe_layer(x1, u_packed, meta, meta_t, counts, mod, next_mod, lng, lnb, w_gu, w_down, layer, seq, xs_init):
    t, d = x1.shape
    tm = 512
    n_rows = t * TOP_K + N_EXPERTS * tm
    n_tiles = n_rows // tm

    counts = counts[0, ROUTER_LANE0:ROUTER_LANE0 + N_EXPERTS].astype(I32)
    tiles_per_expert = (counts + tm - 1) // tm
    tile_end = jnp.cumsum(tiles_per_expert)
    row_start = (tile_end - tiles_per_expert) * tm
    n_used = tile_end[-1:]
    tile_ids = jnp.minimum(jnp.arange(n_tiles, dtype=I32), n_used[0] - 1)
    tile_expert = jnp.sum((tile_end[None, :] <= tile_ids[:, None]).astype(I32), axis=1)
    ids = jnp.arange(N_EXPERTS, dtype=I32)
    later = (ids[None, :] > ids[:, None]) & (tiles_per_expert[None, :] > 0)
    following = jnp.min(jnp.where(later, ids[None, :], N_EXPERTS), axis=1)
    following = jnp.where(following == N_EXPERTS, ids, following)
    next_expert = jnp.sum(jnp.where(tile_expert[:, None] == ids[None, :], following[None, :], 0), axis=1)

    p = _plan(meta_t, row_start)
    if xs_init is None:
        xs_init = jnp.zeros((n_rows, d // 2), I32)
    xs = _dispatch(u_packed, p[0], p[1], xs_init)
    ys = _experts(xs, tile_expert, next_expert, n_used, w_gu, w_down, layer, tm)
    x2, u_next = _combine(x1, meta, mod, next_mod, lng, lnb, ys, p[0], p[1], seq)
    return x2, u_next, xs


def kernel(x, c, ada_w, ada_b, ln_g, ln_b, pool_w, pool_scale, fox_w_in, fox_b_f, fox_w_o,
           router_w_group, router_b_group, router_w_expert, router_b_expert, moe_w_gate_up, moe_w_down):
    bsz, seq, d = x.shape
    mods = _adaln(c, ada_w, ada_b)
    n_mixers = 2
    is_fox = lambda l: l % n_mixers == 1
    u_mixer, xs_buf = None, None
    for l in range(DEPTH):
        mod = mods[l]
        router = _router_weights(router_w_group[l], router_b_group[l], router_w_expert[l], router_b_expert[l])
        j = l // n_mixers
        if is_fox(l):
            assert u_mixer is not None, "an attention layer takes its modulated input from the layer before it"
            x1, u_packed, meta, meta_t, counts = _fox_layer(x.reshape(bsz * seq, d), u_mixer, mod, fox_w_in[j],
                                                            fox_b_f[j], fox_w_o[j], ln_g[l, 0], ln_b[l, 0], router,
                                                            bsz, seq)
        else:
            x1, u_packed, meta, meta_t, counts = _pool_layer(x, mod, pool_w[j], pool_scale[j], ln_g[l, 0],
                                                             ln_b[l, 0], router)
            x1 = x1.reshape(bsz * seq, d)
        next_mod = mods[l + 1] if l + 1 < DEPTH and is_fox(l + 1) else None
        x, u_mixer, xs_buf = _moe_layer(x1, u_packed, meta, meta_t, counts, mod, next_mod, ln_g[l, 1], ln_b[l, 1],
                                        moe_w_gate_up, moe_w_down, l, seq, xs_buf)
        x = x.reshape(bsz, seq, d)
    return x
```

```python
import functools

import jax
import jax.numpy as jnp
from jax import lax
from jax.experimental import pallas as pl
from jax.experimental.pallas import tpu as pltpu

F32 = jnp.float32
BF16 = jnp.bfloat16
I32 = jnp.int32
HIGHEST = lax.Precision.HIGHEST

DEPTH = 2
POOL_WINDOWS = (2, 4, 8, 16)
POOL_HALO = 16
HEAD_DIM = 128
N_GROUPS = 8
EXPERTS_PER_GROUP = 8
N_EXPERTS = N_GROUPS * EXPERTS_PER_GROUP
TOP_K = 2
ALPHA = (2 * DEPTH) ** 0.25
LN_EPS = 1e-5
LOG2E = 1.4426950408889634

LANES = 128
V7X_VMEM_BYTES = 64 * 1024 * 1024
VMEM_LIMIT_BYTES = V7X_VMEM_BYTES - 8 * 1024 * 1024

META_E1, META_E2, META_W1, META_W2, META_R1, META_R2 = range(6)
META_ROWS = 8
ROUTER_LANE0 = N_GROUPS


def _params(*semantics):
    return pltpu.CompilerParams(dimension_semantics=semantics, vmem_limit_bytes=VMEM_LIMIT_BYTES)


def _tile(n, pref):
    t = min(n, pref)
    while n % t:
        t //= 2
    return t


def _adaln_kernel(c_ref, w_ref, b_ref, o_ref):
    c = c_ref[...]
    c_act = c * (1.0 / (1.0 + jnp.exp(-c)))
    o_ref[0] = jnp.dot(c_act, w_ref[0], precision=HIGHEST, preferred_element_type=F32) + b_ref[0]


def _adaln(c, ada_w, ada_b):
    n_layers, d, n = ada_w.shape
    b = c.shape[0]
    rows = -(-b // 8) * 8
    c_pad = jnp.pad(c, ((0, rows - b), (0, 0)))
    tn = _tile(n, 1024)
    out = pl.pallas_call(
        _adaln_kernel,
        grid=(n_layers, n // tn),
        in_specs=[
            pl.BlockSpec((rows, d), lambda l, j: (0, 0)),
            pl.BlockSpec((1, d, tn), lambda l, j: (l, 0, j)),
            pl.BlockSpec((1, 1, tn), lambda l, j: (l, 0, j)),
        ],
        out_specs=pl.BlockSpec((1, rows, tn), lambda l, j: (l, 0, j)),
        out_shape=jax.ShapeDtypeStruct((n_layers, rows, n), F32),
        compiler_params=_params("arbitrary", "arbitrary"),
        name="adaln_mod",
    )(c_pad, ada_w, ada_b.reshape(n_layers, 1, n))
    return out[:, :b].reshape(n_layers, b, 6, d)


def _layer_norm(h, g, b):
    mu = jnp.mean(h, axis=-1, keepdims=True)
    d = h - mu
    var = jnp.mean(d * d, axis=-1, keepdims=True)
    return d * lax.rsqrt(var + LN_EPS) * g + b


def _pack_bf16_pairs(u_hi):
    half = u_hi.shape[1] // 2
    bits = lax.bitcast_convert_type(u_hi.astype(F32), I32)
    return bits[:, :half] | lax.shift_right_logical(bits[:, half:], 16)


def _unpack_pairs_f32(words):
    hi = lax.bitcast_convert_type(words & jnp.int32(-65536), F32)
    lo = lax.bitcast_convert_type(lax.shift_left(words, 16), F32)
    return hi, lo


def _unpack_bf16_pairs(words):
    hi, lo = _unpack_pairs_f32(words)
    return hi.astype(BF16), lo.astype(BF16)


def _route(logits, cnt_ref):
    rows = logits.shape[0]
    lane = lax.broadcasted_iota(I32, (rows, LANES), 1).astype(F32)
    neg_inf = jnp.float32(-jnp.inf)
    big = jnp.float32(2 * LANES)

    g_mask = lane < N_GROUPS
    gl = jnp.where(g_mask, logits, neg_inf)
    g_max = jnp.max(gl, axis=-1, keepdims=True)
    g_idx = jnp.min(jnp.where(gl == g_max, lane, big), axis=-1, keepdims=True)
    p_group = 1.0 / jnp.sum(jnp.where(g_mask, jnp.exp(gl - g_max), 0.0), axis=-1, keepdims=True)

    lo = ROUTER_LANE0 + EXPERTS_PER_GROUP * g_idx
    e_mask = (lane >= lo) & (lane < lo + EXPERTS_PER_GROUP)
    el = jnp.where(e_mask, logits, neg_inf)
    v1 = jnp.max(el, axis=-1, keepdims=True)
    i1 = jnp.min(jnp.where(el == v1, lane, big), axis=-1, keepdims=True)
    el2 = jnp.where(lane == i1, neg_inf, el)
    v2 = jnp.max(el2, axis=-1, keepdims=True)
    i2 = jnp.min(jnp.where(el2 == v2, lane, big), axis=-1, keepdims=True)
    t = jnp.exp(v2 - v1)
    w1 = p_group / (1.0 + t)
    w2 = p_group * t / (1.0 + t)

    sel1 = lane == i1
    sel2 = lane == i2
    onehot = jnp.where(sel1 | sel2, 1.0, 0.0)
    r_idx = lax.broadcasted_iota(I32, (rows, rows), 0)
    c_idx = lax.broadcasted_iota(I32, (rows, rows), 1)
    strict_lower = jnp.where(c_idx < r_idx, 1.0, 0.0).astype(BF16)
    before = jnp.dot(strict_lower, onehot.astype(BF16), preferred_element_type=F32) + cnt_ref[...]
    r1 = jnp.sum(jnp.where(sel1, before, 0.0), axis=-1, keepdims=True)
    r2 = jnp.sum(jnp.where(sel2, before, 0.0), axis=-1, keepdims=True)
    cnt_ref[...] += jnp.sum(onehot, axis=0, keepdims=True)

    cols = (i1 - ROUTER_LANE0, i2 - ROUTER_LANE0, w1, w2, r1, r2)
    meta = jnp.zeros((rows, LANES), F32)
    for k, col in enumerate(cols):
        meta = jnp.where(lane == k, col, meta)
    return meta


def _post_mixer(x, y, mod, lng, lnb, wr_hi, wr_lo, br, x1_ref, up_ref, meta_ref, metat_ref, cnt_ref):
    gate_t, shift_c, scale_c = mod[2:3], mod[3:4], mod[4:5]
    x1 = _layer_norm(ALPHA * x + (1.0 + gate_t) * y, lng, lnb)
    x1_ref[...] = x1
    u = x1 * (1.0 + scale_c) + shift_c
    u_hi = u.astype(BF16)
    u_lo = (u - u_hi.astype(F32)).astype(BF16)
    up_ref[...] = _pack_bf16_pairs(u_hi)
    logits = (jnp.dot(u_hi, wr_hi, preferred_element_type=F32)
              + jnp.dot(u_lo, wr_hi, preferred_element_type=F32)
              + jnp.dot(u_hi, wr_lo, preferred_element_type=F32)) + br
    meta = _route(logits, cnt_ref)
    meta_ref[...] = meta
    metat_ref[...] = meta.T[:META_ROWS, :]


def _router_weights(w_rg, b_rg, w_re, b_re):
    d = w_rg.shape[0]
    pad = LANES - N_GROUPS - N_EXPERTS
    w = jnp.concatenate([w_rg, w_re, jnp.zeros((d, pad), F32)], axis=1)
    b = jnp.concatenate([b_rg, b_re, jnp.zeros((pad,), F32)]).reshape(1, LANES)
    w_hi = w.astype(BF16)
    w_lo = (w - w_hi.astype(F32)).astype(BF16)
    return w_hi, w_lo, b


def _pool_kernel(x_ref, halo_ref, mod_ref, wp_ref, ps_ref, lng_ref, lnb_ref, wrh_ref, wrl_ref, br_ref,
                 x1_ref, up_ref, meta_ref, metat_ref, cnt_ref):
    b = pl.program_id(0)
    i = pl.program_id(1)

    @pl.when((b == 0) & (i == 0))
    def _():
        cnt_ref[...] = jnp.zeros_like(cnt_ref)

    mod = mod_ref[0]
    shift_t, scale_t = mod[0:1], mod[1:2]
    x = x_ref[0]
    rows, d = x.shape
    group = d // len(POOL_WINDOWS)
    u = x * (1.0 + scale_t) + shift_t
    halo = halo_ref[0] * (1.0 + scale_t) + shift_t
    halo = jnp.where(i > 0, halo, 0.0)
    pos = i * rows + lax.broadcasted_iota(I32, (rows, 1), 0)

    ys = []
    for g, w in enumerate(POOL_WINDOWS):
        cols = slice(g * group, (g + 1) * group)
        s = jnp.concatenate([halo[:, cols], u[:, cols]], axis=0)
        k = 1
        while k < w:
            s = s + pltpu.roll(s, k, axis=0)
            k *= 2
        inv_count = 1.0 / jnp.minimum(pos + 1, w).astype(F32)
        pooled = s[POOL_HALO:, :] * inv_count - u[:, cols]
        mixed = jnp.dot(pooled.astype(BF16), wp_ref[g], preferred_element_type=F32)
        ys.append(mixed * ps_ref[:, cols])
    y = jnp.concatenate(ys, axis=-1)

    _post_mixer(x, y, mod, lng_ref[...], lnb_ref[...], wrh_ref[...], wrl_ref[...], br_ref[...],
                x1_ref.at[0], up_ref, meta_ref, metat_ref, cnt_ref)


def _pool_layer(x, mod, pool_w, pool_scale, lng, lnb, router):
    bsz, s, d = x.shape
    ts = _tile(s, 256)
    nsb = s // ts
    hb = ts // POOL_HALO
    wr_hi, wr_lo, br = router
    vec = lambda b, i: (0, 0)
    return pl.pallas_call(
        _pool_kernel,
        grid=(bsz, nsb),
        in_specs=[
            pl.BlockSpec((1, ts, d), lambda b, i: (b, i, 0)),
            pl.BlockSpec((1, POOL_HALO, d), lambda b, i: (b, jnp.maximum(i * hb - 1, 0), 0)),
            pl.BlockSpec((1, 6, d), lambda b, i: (b, 0, 0)),
            pl.BlockSpec(pool_w.shape, lambda b, i: (0, 0, 0)),
            pl.BlockSpec((1, d), vec),
            pl.BlockSpec((1, d), vec),
            pl.BlockSpec((1, d), vec),
            pl.BlockSpec((d, LANES), vec),
            pl.BlockSpec((d, LANES), vec),
            pl.BlockSpec((1, LANES), vec),
        ],
        out_specs=[
            pl.BlockSpec((1, ts, d), lambda b, i: (b, i, 0)),
            pl.BlockSpec((ts, d // 2), lambda b, i: (b * nsb + i, 0)),
            pl.BlockSpec((ts, LANES), lambda b, i: (b * nsb + i, 0)),
            pl.BlockSpec((META_ROWS, ts), lambda b, i: (0, b * nsb + i)),
            pl.BlockSpec((1, LANES), vec),
        ],
        out_shape=[
            jax.ShapeDtypeStruct((bsz, s, d), F32),
            jax.ShapeDtypeStruct((bsz * s, d // 2), I32),
            jax.ShapeDtypeStruct((bsz * s, LANES), F32),
            jax.ShapeDtypeStruct((META_ROWS, bsz * s), F32),
            jax.ShapeDtypeStruct((1, LANES), F32),
        ],
        compiler_params=_params("arbitrary", "arbitrary"),
        name="pool_layer",
    )(x, x, mod, pool_w.astype(BF16), pool_scale.reshape(1, d), lng.reshape(1, d), lnb.reshape(1, d),
      wr_hi, wr_lo, br)


def _fox_proj_kernel(u_ref, w_ref, wf_ref, bf_ref, qkv_ref, cum_ref, carry_scr, *, tiles_per_seq):
    j = pl.program_id(0)
    i = pl.program_id(1)
    u = u_ref[...]

    @pl.when((j == 0) & (i % tiles_per_seq == 0))
    def _():
        carry_scr[...] = jnp.zeros_like(carry_scr)

    @pl.when(j == 0)
    def _():
        f_logit = jnp.dot(u, wf_ref[...], preferred_element_type=F32) + bf_ref[...]
        log_f = jnp.minimum(f_logit, 0.0) - jnp.log1p(jnp.exp(-jnp.abs(f_logit)))
        rows = log_f.shape[0]
        r_idx = lax.broadcasted_iota(I32, (rows, rows), 0)
        c_idx = lax.broadcasted_iota(I32, (rows, rows), 1)
        lower = jnp.where(c_idx <= r_idx, 1.0, 0.0).astype(BF16)
        cum = carry_scr[...]
        for piece in _split3(log_f):
            cum = cum + jnp.dot(lower, piece.astype(BF16), preferred_element_type=F32)
        cum_ref[...] = cum
        carry_scr[...] = cum[rows - 1:rows, :]

    qkv_ref[...] = jnp.dot(u, w_ref[...], preferred_element_type=F32).astype(BF16)


def _fox_proj(u, w_qkv, w_f, b_f, seq):
    t, d = u.shape
    n = w_qkv.shape[1]
    tm = _tile(seq, 512)
    tn = _tile(n, 1536)
    tiles_per_seq = seq // tm
    n_i = t // tm
    return pl.pallas_call(
        functools.partial(_fox_proj_kernel, tiles_per_seq=tiles_per_seq),
        grid=(n // tn, n_i),
        in_specs=[
            pl.BlockSpec((tm, d), lambda j, i: (i, 0)),
            pl.BlockSpec((d, tn), lambda j, i: (0, j)),
            pl.BlockSpec((d, LANES), lambda j, i: (0, 0)),
            pl.BlockSpec((1, LANES), lambda j, i: (0, 0)),
        ],
        out_specs=[
            pl.BlockSpec((tm, tn), lambda j, i: (i, j)),
            pl.BlockSpec((tm, LANES), lambda j, i: (jnp.where(j == 0, i, n_i - 1), 0)),
        ],
        out_shape=[
            jax.ShapeDtypeStruct((t, n), BF16),
            jax.ShapeDtypeStruct((t, LANES), F32),
        ],
        scratch_shapes=[pltpu.VMEM((1, LANES), F32)],
        compiler_params=_params("arbitrary", "arbitrary"),
        name="fox_proj",
    )(u, w_qkv, w_f, b_f)


def _split3(col):
    hi = col.astype(BF16).astype(F32)
    rest = col - hi
    mid = rest.astype(BF16).astype(F32)
    lo = (rest - mid).astype(BF16).astype(F32)
    return hi, mid, lo


def _head_column(block, h):
    lane = lax.broadcasted_iota(I32, block.shape, 1)
    return jnp.sum(jnp.where(lane == h, block, 0.0), axis=-1, keepdims=True) * LOG2E


def _fox_attn_kernel(q_ref, k_ref, v_ref, cq_ref, ck_ref, o_ref,
                     qa_scr, kx_scr, vx_scr, s_a, s_b, rm_a, rm_b, m_scr, acc_scr, *, build_rows):
    h = pl.program_id(1)
    qi = pl.program_id(2)
    tq = q_ref.shape[0]
    tk = vx_scr.shape[0]
    seq = k_ref.shape[0]
    lane = lax.broadcasted_iota(I32, (build_rows, LANES), 1)

    @pl.when(qi == 0)
    def _():
        def build(i, _):
            start = pl.multiple_of(i * build_rows, build_rows)
            hi, mid, lo = _split3(_head_column(ck_ref[pl.ds(start, build_rows), :], h))
            kx = jnp.where(lane < 3, 1.0,
                           jnp.where(lane == 3, -hi, jnp.where(lane == 4, -mid, jnp.where(lane == 5, -lo, 0.0))))
            kx_scr[pl.ds(start, build_rows), :] = kx.astype(BF16)
            return 0

        lax.fori_loop(0, seq // build_rows, build, 0)
        lane_v = lax.broadcasted_iota(I32, (tk, LANES), 1)
        vx_scr[...] = jnp.where(lane_v == 0, 1.0, 0.0).astype(BF16)

    lane_q = lax.broadcasted_iota(I32, (tq, LANES), 1)
    hi, mid, lo = _split3(_head_column(cq_ref[...], h))
    qx = jnp.where(lane_q == 0, hi, jnp.where(lane_q == 1, mid, jnp.where(lane_q == 2, lo,
                                                                            jnp.where(lane_q < 6, 1.0, 0.0))))
    qa_scr[:, :HEAD_DIM] = q_ref[...]
    qa_scr[:, HEAD_DIM:] = qx.astype(BF16)
    m_scr[...] = jnp.full(m_scr.shape, -jnp.inf, F32)
    acc_scr[...] = jnp.zeros(acc_scr.shape, F32)

    def produce(c, s_dst, rm_dst, masked):
        start = pl.multiple_of(c * tk, tk)
        kk = jnp.concatenate([k_ref[pl.ds(start, tk), :], kx_scr[pl.ds(start, tk), :]], axis=1)
        s = lax.dot_general(qa_scr[...], kk, (((1,), (1,)), ((), ())), preferred_element_type=F32)
        if masked:
            q_pos = qi * tq + lax.broadcasted_iota(I32, (tq, tk), 0)
            k_pos = start + lax.broadcasted_iota(I32, (tq, tk), 1)
            s = jnp.where(k_pos <= q_pos, s, -jnp.inf)
        s_dst[...] = s
        rm_dst[...] = jnp.broadcast_to(jnp.max(s, axis=-1, keepdims=True), rm_dst.shape)

    def consume(c, s_src, rm_src):
        start = pl.multiple_of(c * tk, tk)
        vv = jnp.concatenate([v_ref[pl.ds(start, tk), :], vx_scr[...]], axis=1)
        m_old = m_scr[...]
        m_new = jnp.maximum(m_old, rm_src[...])
        alpha = jnp.exp2(m_old - m_new)
        m_scr[...] = m_new
        p = jnp.concatenate(
            [jnp.exp2(s_src[:, j * LANES:(j + 1) * LANES] - m_new).astype(BF16) for j in range(tk // LANES)],
            axis=1)
        pv = jnp.dot(p, vv, preferred_element_type=F32)
        for j in range(2 * HEAD_DIM // LANES):
            blk = slice(j * LANES, (j + 1) * LANES)
            acc_scr[:, blk] = alpha * acc_scr[:, blk] + pv[:, blk]

    produce(0, s_a, rm_a, True)
    produce(1, s_b, rm_b, True)

    def pair(j, masked):
        c = 2 * j
        consume(c, s_a, rm_a)
        produce(c + 2, s_a, rm_a, masked)
        consume(c + 1, s_b, rm_b)
        produce(c + 3, s_b, rm_b, masked)

    def two_unmasked_pairs(jj, _):
        pair(2 * jj, False)
        pair(2 * jj + 1, False)
        return 0

    n_unmasked = jnp.maximum(qi - 1, 0)
    lax.fori_loop(0, n_unmasked // 2, two_unmasked_pairs, 0)

    @pl.when(n_unmasked % 2 == 1)
    def _():
        pair(n_unmasked - 1, False)

    @pl.when(qi > 0)
    def _():
        pair(qi - 1, True)

    consume(2 * qi, s_a, rm_a)
    consume(2 * qi + 1, s_b, rm_b)

    acc = acc_scr[...]
    o_ref[...] = (acc[:, :HEAD_DIM] / acc[:, HEAD_DIM:HEAD_DIM + 1]).astype(BF16)


def _fox_attn(qkv, cum, bsz, seq, d):
    heads = d // HEAD_DIM
    tq = _tile(seq, 1024)
    tk = tq // 2
    nq = seq // tq
    return pl.pallas_call(
        functools.partial(_fox_attn_kernel, build_rows=tk),
        grid=(bsz, heads, nq),
        in_specs=[
            pl.BlockSpec((tq, HEAD_DIM), lambda b, h, i: (b * nq + i, h)),
            pl.BlockSpec((seq, HEAD_DIM), lambda b, h, i: (b, heads + h)),
            pl.BlockSpec((seq, HEAD_DIM), lambda b, h, i: (b, 2 * heads + h)),
            pl.BlockSpec((tq, LANES), lambda b, h, i: (b * nq + i, 0)),
            pl.BlockSpec((seq, LANES), lambda b, h, i: (b, 0)),
        ],
        out_specs=pl.BlockSpec((tq, HEAD_DIM), lambda b, h, i: (b * nq + i, h)),
        out_shape=jax.ShapeDtypeStruct((bsz * seq, d), BF16),
        scratch_shapes=[
            pltpu.VMEM((tq, 2 * HEAD_DIM), BF16),
            pltpu.VMEM((seq, LANES), BF16),
            pltpu.VMEM((tk, LANES), BF16),
            pltpu.VMEM((tq, tk), F32),
            pltpu.VMEM((tq, tk), F32),
            pltpu.VMEM((tq, LANES), F32),
            pltpu.VMEM((tq, LANES), F32),
            pltpu.VMEM((tq, LANES), F32),
            pltpu.VMEM((tq, 2 * HEAD_DIM), F32),
        ],
        compiler_params=_params("arbitrary", "arbitrary", "arbitrary"),
        name="fox_attn",
    )(qkv, qkv, qkv, cum, cum)


def _fox_out_kernel(a_ref, x_ref, mod_ref, wo_ref, lng_ref, lnb_ref, wrh_ref, wrl_ref, br_ref,
                    x1_ref, up_ref, meta_ref, metat_ref, cnt_ref):
    @pl.when(pl.program_id(0) == 0)
    def _():
        cnt_ref[...] = jnp.zeros_like(cnt_ref)

    y = jnp.dot(a_ref[...], wo_ref[...], preferred_element_type=F32)
    _post_mixer(x_ref[...], y, mod_ref[0], lng_ref[...], lnb_ref[...], wrh_ref[...], wrl_ref[...],
                br_ref[...], x1_ref, up_ref, meta_ref, metat_ref, cnt_ref)


def _fox_out(attn, x2d, mod, w_o, lng, lnb, router, seq):
    t, d = x2d.shape
    tm = _tile(seq, 256)
    tiles_per_seq = seq // tm
    wr_hi, wr_lo, br = router
    vec = lambda i: (0, 0)
    return pl.pallas_call(
        _fox_out_kernel,
        grid=(t // tm,),
        in_specs=[
            pl.BlockSpec((tm, d), lambda i: (i, 0)),
            pl.BlockSpec((tm, d), lambda i: (i, 0)),
            pl.BlockSpec((1, 6, d), lambda i: (i // tiles_per_seq, 0, 0)),
            pl.BlockSpec((d, d), vec),
            pl.BlockSpec((1, d), vec),
            pl.BlockSpec((1, d), vec),
            pl.BlockSpec((d, LANES), vec),
            pl.BlockSpec((d, LANES), vec),
            pl.BlockSpec((1, LANES), vec),
        ],
        out_specs=[
            pl.BlockSpec((tm, d), lambda i: (i, 0)),
            pl.BlockSpec((tm, d // 2), lambda i: (i, 0)),
            pl.BlockSpec((tm, LANES), lambda i: (i, 0)),
            pl.BlockSpec((META_ROWS, tm), lambda i: (0, i)),
            pl.BlockSpec((1, LANES), vec),
        ],
        out_shape=[
            jax.ShapeDtypeStruct((t, d), F32),
            jax.ShapeDtypeStruct((t, d // 2), I32),
            jax.ShapeDtypeStruct((t, LANES), F32),
            jax.ShapeDtypeStruct((META_ROWS, t), F32),
            jax.ShapeDtypeStruct((1, LANES), F32),
        ],
        compiler_params=_params("arbitrary"),
        name="fox_out",
    )(attn, x2d, mod, w_o, lng.reshape(1, d), lnb.reshape(1, d), wr_hi, wr_lo, br)


def _fox_layer(x2d, u, mod, w_in, b_f, w_o, lng, lnb, router, bsz, seq):
    t, d = x2d.shape
    heads = d // HEAD_DIM
    q_scale = HEAD_DIM ** -0.5 * LOG2E
    w_qkv = jnp.concatenate([w_in[:, :d] * q_scale, w_in[:, d:3 * d]], axis=1).astype(BF16)
    w_f = jnp.pad(w_in[:, 3 * d:], ((0, 0), (0, LANES - heads))).astype(BF16)
    b_f = jnp.pad(b_f, (0, LANES - heads)).reshape(1, LANES)
    qkv, cum = _fox_proj(u, w_qkv, w_f, b_f, seq)
    attn = _fox_attn(qkv, cum, bsz, seq, d)
    return _fox_out(attn, x2d, mod, w_o.astype(BF16), lng, lnb, router, seq)


def _row_copy(src_ref, src_row, dst_ref, dst_row, sem):
    return pltpu.make_async_copy(src_ref.at[pl.ds(src_row, 1)], dst_ref.at[pl.ds(dst_row, 1)], sem)


def _plan_kernel(mt_ref, rs_ref, p_ref):
    mt = mt_ref[...]
    rs = rs_ref[...]
    expert = lax.broadcasted_iota(I32, (N_EXPERTS, mt.shape[1]), 0).astype(F32)
    for k, (e_row, r_row) in enumerate(((META_E1, META_R1), (META_E2, META_R2))):
        start = jnp.sum(jnp.where(expert == mt[e_row:e_row + 1], rs, 0.0), axis=0, keepdims=True)
        p_ref[k:k + 1, :] = (start + mt[r_row:r_row + 1]).astype(I32)


def _plan(meta_t, row_start):
    t = meta_t.shape[1]
    tn = _tile(t, 4096)
    return pl.pallas_call(
        _plan_kernel,
        grid=(t // tn,),
        in_specs=[
            pl.BlockSpec((META_ROWS, tn), lambda i: (0, i)),
            pl.BlockSpec((N_EXPERTS, 1), lambda i: (0, 0)),
        ],
        out_specs=pl.BlockSpec((TOP_K, tn), lambda i: (0, i)),
        out_shape=jax.ShapeDtypeStruct((TOP_K, t), I32),
        compiler_params=_params("arbitrary"),
        name="moe_plan",
    )(meta_t, row_start.astype(F32).reshape(N_EXPERTS, 1))


def _dispatch_kernel(p1_ref, p2_ref, u_ref, xs_in_ref, xs_ref, sem):
    del xs_in_ref
    rows = u_ref.shape[0]
    base = pl.program_id(0) * rows

    def issue(r, _):
        _row_copy(u_ref, r, xs_ref, p1_ref[base + r], sem).start(priority=0)
        _row_copy(u_ref, r, xs_ref, p2_ref[base + r], sem).start(priority=1)
        return 0

    lax.fori_loop(0, rows, issue, 0, unroll=8)

    def drain(r, _):
        _row_copy(u_ref, 0, xs_ref, 0, sem).wait()
        _row_copy(u_ref, 0, xs_ref, 0, sem).wait()
        return 0

    lax.fori_loop(0, rows, drain, 0, unroll=8)


def _dispatch(u_packed, p1, p2, xs_init):
    t, half = u_packed.shape
    ts = _tile(t, 512)
    grid_spec = pltpu.PrefetchScalarGridSpec(
        num_scalar_prefetch=2,
        grid=(t // ts,),
        in_specs=[
            pl.BlockSpec((ts, half), lambda i, p1, p2: (i, 0)),
            pl.BlockSpec(memory_space=pl.ANY),
        ],
        out_specs=pl.BlockSpec(memory_space=pl.ANY),
        scratch_shapes=[pltpu.SemaphoreType.DMA],
    )
    return pl.pallas_call(
        _dispatch_kernel,
        grid_spec=grid_spec,
        out_shape=jax.ShapeDtypeStruct(xs_init.shape, I32),
        input_output_aliases={3: 0},
        compiler_params=_params("arbitrary"),
        name="moe_dispatch",
    )(p1, p2, u_packed, xs_init)


def _experts_kernel(te_ref, ne_ref, nu_ref, xs_ref, wgu_hbm, wd_hbm, o_ref, wgu_st, wd_st, wgu_bf, wd_bf, sems, *,
                    layer):
    i = pl.program_id(0)
    expert = te_ref[i]
    prev = te_ref[jnp.maximum(i - 1, 0)]
    fresh = ((i == 0) | (expert != prev)) & (i < nu_ref[0])

    def fetch(e):
        return (pltpu.make_async_copy(wgu_hbm.at[layer, e], wgu_st, sems.at[0]),
                pltpu.make_async_copy(wd_hbm.at[layer, e], wd_st, sems.at[1]))

    @pl.when(i == 0)
    def _():
        for copy in fetch(expert):
            copy.start()

    @pl.when(fresh)
    def _():
        for copy in fetch(expert):
            copy.wait()
        wgu_bf[...] = wgu_st[...].astype(BF16)
        wd_bf[...] = wd_st[...].astype(BF16)
        following = ne_ref[i]

        @pl.when(following != expert)
        def _():
            for copy in fetch(following):
                copy.start()

    @pl.when(i < nu_ref[0])
    def _():
        hi, lo = _unpack_bf16_pairs(xs_ref[...])
        half = hi.shape[1]
        f = wd_bf.shape[0]
        gu = (jnp.dot(hi, wgu_bf[:half, :], preferred_element_type=F32)
              + jnp.dot(lo, wgu_bf[half:, :], preferred_element_type=F32))
        gate, up = gu[:, :f], gu[:, f:]
        act = gate * (1.0 / (1.0 + jnp.exp(-gate))) * up
        out = jnp.dot(act.astype(BF16), wd_bf[...], preferred_element_type=F32)
        o_ref[...] = _pack_bf16_pairs(out.astype(BF16))

    @pl.when(i >= nu_ref[0])
    def _():
        o_ref[...] = jnp.zeros_like(o_ref)


def _experts(xs, tile_expert, next_expert, n_used, w_gu, w_down, layer, tm):
    n_rows, half = xs.shape
    _, _, d, f2 = w_gu.shape
    f = f2 // 2
    n_tiles = n_rows // tm
    grid_spec = pltpu.PrefetchScalarGridSpec(
        num_scalar_prefetch=3,
        grid=(n_tiles,),
        in_specs=[
            pl.BlockSpec((tm, half), lambda i, te, ne, nu: (jnp.minimum(i, nu[0] - 1), 0)),
            pl.BlockSpec(memory_space=pl.ANY),
            pl.BlockSpec(memory_space=pl.ANY),
        ],
        out_specs=pl.BlockSpec((tm, half), lambda i, te, ne, nu: (i, 0)),
        scratch_shapes=[pltpu.VMEM((d, f2), F32), pltpu.VMEM((f, d), F32),
                        pltpu.VMEM((d, f2), BF16), pltpu.VMEM((f, d), BF16),
                        pltpu.SemaphoreType.DMA((2,))],
    )
    return pl.pallas_call(
        functools.partial(_experts_kernel, layer=layer),
        grid_spec=grid_spec,
        out_shape=jax.ShapeDtypeStruct((n_rows, half), I32),
        compiler_params=_params("arbitrary"),
        name="moe_experts",
    )(tile_expert, next_expert, n_used, xs, w_gu, w_down)


def _combine_kernel(p1_ref, p2_ref, x_ref, meta_ref, mod_ref, lng_ref, lnb_ref, ys_ref, *rest, emit_next):
    if emit_next:
        nmod_ref, o_ref, un_ref, r1, r2, sems = rest
    else:
        o_ref, r1, r2, sems = rest
    i = pl.program_id(0)
    rows = x_ref.shape[0]

    def gather(step, slot):
        base = step * rows

        def issue(r, _):
            _row_copy(ys_ref, p1_ref[base + r], r1.at[slot], r, sems.at[slot]).start(priority=0)
            _row_copy(ys_ref, p2_ref[base + r], r2.at[slot], r, sems.at[slot]).start(priority=1)
            return 0

        lax.fori_loop(0, rows, issue, 0, unroll=8)

    @pl.when(i == 0)
    def _():
        gather(0, 0)

    @pl.when(i + 1 < pl.num_programs(0))
    def _():
        gather(i + 1, (i + 1) % 2)

    slot = i % 2

    def drain(r, _):
        _row_copy(ys_ref, 0, r1.at[slot], 0, sems.at[slot]).wait()
        _row_copy(ys_ref, 0, r2.at[slot], 0, sems.at[slot]).wait()
        return 0

    lax.fori_loop(0, rows, drain, 0, unroll=8)

    meta = meta_ref[...]
    w1 = meta[:, META_W1:META_W1 + 1]
    w2 = meta[:, META_W2:META_W2 + 1]
    hi1, lo1 = _unpack_pairs_f32(r1[slot])
    hi2, lo2 = _unpack_pairs_f32(r2[slot])
    y = jnp.concatenate([w1 * hi1 + w2 * hi2, w1 * lo1 + w2 * lo2], axis=1)
    gate_c = mod_ref[0][5:6]
    x2 = _layer_norm(ALPHA * x_ref[...] + (1.0 + gate_c) * y, lng_ref[...], lnb_ref[...])
    o_ref[...] = x2
    if emit_next:
        nmod = nmod_ref[0]
        un_ref[...] = (x2 * (1.0 + nmod[1:2]) + nmod[0:1]).astype(BF16)


def _combine(x1, meta, mod, next_mod, lng, lnb, ys, p1, p2, seq):
    t, d = x1.shape
    ts = _tile(seq, 256)
    tiles_per_seq = seq // ts
    emit_next = next_mod is not None
    vec = lambda i, p1, p2: (0, 0)
    per_seq = lambda i, p1, p2: (i // tiles_per_seq, 0, 0)
    rows = lambda i, p1, p2: (i, 0)
    in_specs = [
        pl.BlockSpec((ts, d), rows),
        pl.BlockSpec((ts, LANES), rows),
        pl.BlockSpec((1, 6, d), per_seq),
        pl.BlockSpec((1, d), vec),
        pl.BlockSpec((1, d), vec),
        pl.BlockSpec(memory_space=pl.ANY),
    ]
    args = [x1, meta, mod, lng.reshape(1, d), lnb.reshape(1, d), ys]
    out_specs = [pl.BlockSpec((ts, d), rows)]
    out_shape = [jax.ShapeDtypeStruct((t, d), F32)]
    if emit_next:
        in_specs.append(pl.BlockSpec((1, 6, d), per_seq))
        args.append(next_mod)
        out_specs.append(pl.BlockSpec((ts, d), rows))
        out_shape.append(jax.ShapeDtypeStruct((t, d), BF16))
    grid_spec = pltpu.PrefetchScalarGridSpec(
        num_scalar_prefetch=2,
        grid=(t // ts,),
        in_specs=in_specs,
        out_specs=out_specs,
        scratch_shapes=[pltpu.VMEM((2, ts, d // 2), I32), pltpu.VMEM((2, ts, d // 2), I32),
                        pltpu.SemaphoreType.DMA((2,))],
    )
    outs = pl.pallas_call(
        functools.partial(_combine_kernel, emit_next=emit_next),
        grid_spec=grid_spec,
        out_shape=out_shape,
        compiler_params=_params("arbitrary"),
        name="moe_combine",
    )(p1, p2, *args)
    return (outs[0], outs[1]) if emit_next else (outs[0], None)


def _moe_layer(x1, u_packed, meta, meta_t, counts, mod, next_mod, lng, lnb, w_gu, w_down, layer, seq, xs_init):
    t, d = x1.shape
    tm = 256
    n_rows = t * TOP_K + N_EXPERTS * tm
    n_tiles = n_rows // tm

    counts = counts[0, ROUTER_LANE0:ROUTER_LANE0 + N_EXPERTS].astype(I32)
    tiles_per_expert = (counts + tm - 1) // tm
    tile_end = jnp.cumsum(tiles_per_expert)
    row_start = (tile_end - tiles_per_expert) * tm
    n_used = tile_end[-1:]
    tile_ids = jnp.minimum(jnp.arange(n_tiles, dtype=I32), n_used[0] - 1)
    tile_expert = jnp.sum((tile_end[None, :] <= tile_ids[:, None]).astype(I32), axis=1)
    ids = jnp.arange(N_EXPERTS, dtype=I32)
    later = (ids[None, :] > ids[:, None]) & (tiles_per_expert[None, :] > 0)
    following = jnp.min(jnp.where(later, ids[None, :], N_EXPERTS), axis=1)
    following = jnp.where(following == N_EXPERTS, ids, following)
    next_expert = jnp.sum(jnp.where(tile_expert[:, None] == ids[None, :], following[None, :], 0), axis=1)

    p = _plan(meta_t, row_start)
    if xs_init is None:
        xs_init = jnp.zeros((n_rows, d // 2), I32)
    xs = _dispatch(u_packed, p[0], p[1], xs_init)
    ys = _experts(xs, tile_expert, next_expert, n_used, w_gu, w_down, layer, tm)
    x2, u_next = _combine(x1, meta, mod, next_mod, lng, lnb, ys, p[0], p[1], seq)
    return x2, u_next, xs


def kernel(x, c, ada_w, ada_b, ln_g, ln_b, pool_w, pool_scale, fox_w_in, fox_b_f, fox_w_o,
           router_w_group, router_b_group, router_w_expert, router_b_expert, moe_w_gate_up, moe_w_down):
    bsz, seq, d = x.shape
    mods = _adaln(c, ada_w, ada_b)
    n_mixers = 2
    is_fox = lambda l: l % n_mixers == 1
    u_mixer, xs_buf = None, None
    for l in range(DEPTH):
        mod = mods[l]
        router = _router_weights(router_w_group[l], router_b_group[l], router_w_expert[l], router_b_expert[l])
        j = l // n_mixers
        if is_fox(l):
            assert u_mixer is not None, "an attention layer takes its modulated input from the layer before it"
            x1, u_packed, meta, meta_t, counts = _fox_layer(x.reshape(bsz * seq, d), u_mixer, mod, fox_w_in[j],
                                                            fox_b_f[j], fox_w_o[j], ln_g[l, 0], ln_b[l, 0], router,
                                                            bsz, seq)
        else:
            x1, u_packed, meta, meta_t, counts = _pool_layer(x, mod, pool_w[j], pool_scale[j], ln_g[l, 0],
                                                             ln_b[l, 0], router)
            x1 = x1.reshape(bsz * seq, d)
        next_mod = mods[l + 1] if l + 1 < DEPTH and is_fox(l + 1) else None
        x, u_mixer, xs_buf = _moe_layer(x1, u_packed, meta, meta_t, counts, mod, next_mod, ln_g[l, 1], ln_b[l, 1],
                                        moe_w_gate_up, moe_w_down, l, seq, xs_buf)
        x = x.reshape(bsz, seq, d)
    return x
```
